```python
import math
import jax, jax.numpy as jnp
from jax import lax
import numpy as np

D_MODEL = 1024
BATCH = 4
SEQ = 4096
DEPTH = 1
DEC_BATCH = 32
DEC_SEQ = 4
PAST_LEN = 16384
PAGE_SIZE = 128

HEAD_DIM = 64
D_SB = D_MODEL // 2
H_SB = D_SB // HEAD_DIM
CH_MLP = 128
D_MLP = D_MODEL - D_SB
G_MLP = D_MLP // CH_MLP
D_MIX = D_SB + D_MLP
D_IN = 3 * D_SB + 2 * D_MLP
CHUNK = 128
Q_BLOCK = 128
D_FF = 2816
CONV_W = 3
N_MOD = 6
SB_BIAS_INIT = -8.0
EPS = 1e-6

kernel_name = 'hymba_style_stickbreak_gmlp_convffn_step'


def rms_norm(x, g):
    xf = x.astype(jnp.float32)
    y = xf * lax.rsqrt(jnp.mean(xf * xf, axis=-1, keepdims=True) + EPS)
    return (y * g.astype(jnp.float32)).astype(x.dtype)


def _sb_block(q_blk, qpos_blk, k, v, k_pos, sb_bias):
    z = jnp.einsum('bqhd,bkhd->bhqk', q_blk.astype(jnp.float32), k.astype(jnp.float32)) * (HEAD_DIM ** -0.5)
    z = z + sb_bias.astype(jnp.float32)[None, :, None, None]
    mask = k_pos[None, :] < qpos_blk[:, None]
    log1m = jnp.where(mask, jax.nn.log_sigmoid(-z), 0.0)
    suffix = lax.cumsum(log1m, axis=3, reverse=True)
    between = jnp.concatenate([suffix[..., 1:], jnp.zeros_like(suffix[..., :1])], axis=-1)
    a = jnp.where(mask, jnp.exp(jax.nn.log_sigmoid(z) + between), 0.0)
    return jnp.einsum('bhqk,bkhd->bqhd', a, v.astype(jnp.float32)).astype(v.dtype)


def stick_breaking(q, k, v, q_pos, k_pos, sb_bias):
    b, tq, h, dh = q.shape
    qb = min(Q_BLOCK, tq)
    nb = -(-tq // qb)
    pad = nb * qb - tq
    qp = jnp.pad(q, ((0, 0), (0, pad), (0, 0), (0, 0)))
    pp = jnp.pad(q_pos, (0, pad))
    qs = qp.reshape(b, nb, qb, h, dh).transpose(1, 0, 2, 3, 4)
    ps = pp.reshape(nb, qb)
    out = lax.map(lambda a: _sb_block(a[0], a[1], k, v, k_pos, sb_bias), (qs, ps))
    out = out.transpose(1, 0, 2, 3, 4).reshape(b, nb * qb, h, dh)
    return out[:, :tq]


def chunk_spatial_gate(u, vn, w_s, b_s):
    b, t, g, c = u.shape
    nc = -(-t // CHUNK)
    pad = nc * CHUNK - t
    vp = jnp.pad(vn, ((0, 0), (0, pad), (0, 0), (0, 0))).reshape(b, nc, CHUNK, g, c)
    causal = jnp.tril(jnp.ones((CHUNK, CHUNK), dtype=bool))
    w = jnp.where(causal[None], w_s, 0.0)
    mixed = jnp.einsum('gts,bnsgc->bntgc', w, vp) + b_s.T[None, None, :, :, None]
    mixed = mixed.reshape(b, nc * CHUNK, g, c)[:, :t]
    return u * mixed


def decoder_layer(x, c, past_k, past_v, conv_buf, q_pos, k_pos,
                  norm1_g, norm2_g, w_ada, b_ada, w_in, q_norm_g, k_norm_g, sb_bias, v_norm_g,
                  w_s, b_s, w_out, w_up, conv_w, conv_b, w_down):
    bsz, t, _ = x.shape
    mod = (jax.nn.silu(c) @ w_ada + b_ada).reshape(bsz, N_MOD, 1, D_MODEL)
    shift1, scale1, gate1, shift2, scale2, gate2 = [mod[:, i] for i in range(N_MOD)]

    h = rms_norm(x, norm1_g) * (1 + scale1) + shift1
    proj = h @ w_in
    q, k, v, u, vg = jnp.split(proj, [D_SB, 2 * D_SB, 3 * D_SB, 3 * D_SB + D_MLP], axis=-1)
    q = rms_norm(q.reshape(bsz, t, H_SB, HEAD_DIM), q_norm_g)
    k = rms_norm(k.reshape(bsz, t, H_SB, HEAD_DIM), k_norm_g)
    v = v.reshape(bsz, t, H_SB, HEAD_DIM)
    if past_k is None:
        k_all, v_all = k, v
    else:
        k_all = jnp.concatenate([past_k.astype(k.dtype), k], axis=1)
        v_all = jnp.concatenate([past_v.astype(v.dtype), v], axis=1)
    o_sb = stick_breaking(q, k_all, v_all, q_pos, k_pos, sb_bias).reshape(bsz, t, D_SB)

    u = jax.nn.gelu(u).reshape(bsz, t, G_MLP, CH_MLP)
    vn = rms_norm(jax.nn.gelu(vg).reshape(bsz, t, G_MLP, CH_MLP), v_norm_g)
    o_mlp = chunk_spatial_gate(u, vn, w_s, b_s).reshape(bsz, t, D_MLP)

    x = x + gate1 * (jnp.concatenate([o_sb, o_mlp], axis=-1) @ w_out)

    h2 = rms_norm(x, norm2_g) * (1 + scale2) + shift2
    up = h2 @ w_up
    buf = jnp.concatenate([conv_buf.astype(up.dtype), up], axis=1)
    conv = conv_b + sum(conv_w[i] * buf[:, i:i + t] for i in range(CONV_W))
    g_ff, v_ff = jnp.split(conv, 2, axis=-1)
    x = x + gate2 * ((jax.nn.silu(g_ff) * v_ff) @ w_down)
    return x, k, v, vn, buf[:, -(CONV_W - 1):]


def setup_inputs(seed: int = 0) -> dict:
    key = jax.random.key(seed)
    ks = jax.random.split(key, 24)
    n_pages = PAST_LEN // PAGE_SIZE
    n_used = DEC_BATCH * n_pages
    n_pool = n_used + (n_used + 3) // 4
    nrm = jax.random.normal
    f32 = jnp.float32
    return {
        'x_prompt': nrm(ks[0], (BATCH, SEQ, D_MODEL), f32),
        'x_sample': nrm(ks[1], (DEC_BATCH, DEC_SEQ, D_MODEL), f32),
        'c_prompt': nrm(ks[2], (BATCH, D_MODEL), f32),
        'c_sample': nrm(ks[3], (DEC_BATCH, D_MODEL), f32),
        'cache_k': nrm(ks[4], (DEPTH, n_pool, PAGE_SIZE, H_SB, HEAD_DIM), f32),
        'cache_v': nrm(ks[5], (DEPTH, n_pool, PAGE_SIZE, H_SB, HEAD_DIM), f32),
        'state_conv': nrm(ks[6], (DEPTH, DEC_BATCH, CONV_W - 1, 2 * D_FF), f32),
        'page_table': jax.random.permutation(ks[7], n_pool)[:n_used].reshape(DEC_BATCH, n_pages).astype(jnp.int32),
        'norm1_g': 1.0 + 0.02 * nrm(ks[8], (DEPTH, D_MODEL), f32),
        'norm2_g': 1.0 + 0.02 * nrm(ks[9], (DEPTH, D_MODEL), f32),
        'w_ada': 0.5 * D_MODEL ** -0.5 * nrm(ks[10], (DEPTH, D_MODEL, N_MOD * D_MODEL), f32),
        'b_ada': 0.02 * nrm(ks[11], (DEPTH, N_MOD * D_MODEL), f32),
        'w_in': D_MODEL ** -0.5 * nrm(ks[12], (DEPTH, D_MODEL, D_IN), f32),
        'q_norm_g': 1.0 + 0.02 * nrm(ks[13], (DEPTH, HEAD_DIM), f32),
        'k_norm_g': 1.0 + 0.02 * nrm(ks[14], (DEPTH, HEAD_DIM), f32),
        'sb_bias': SB_BIAS_INIT + 0.5 * nrm(ks[23], (DEPTH, H_SB), f32),
        'v_norm_g': 1.0 + 0.02 * nrm(ks[15], (DEPTH, G_MLP, CH_MLP), f32),
        'w_s': CHUNK ** -0.5 * nrm(ks[16], (DEPTH, G_MLP, CHUNK, CHUNK), f32),
        'b_s': 1.0 + 0.1 * nrm(ks[17], (DEPTH, G_MLP, CHUNK), f32),
        'w_out': D_MIX ** -0.5 * nrm(ks[18], (DEPTH, D_MIX, D_MODEL), f32),
        'w_up': D_MODEL ** -0.5 * nrm(ks[19], (DEPTH, D_MODEL, 2 * D_FF), f32),
        'conv_w': CONV_W ** -0.5 * nrm(ks[20], (DEPTH, CONV_W, 2 * D_FF), f32),
        'conv_b': 0.02 * nrm(ks[21], (DEPTH, 2 * D_FF), f32),
        'w_down': D_FF ** -0.5 * nrm(ks[22], (DEPTH, D_FF, D_MODEL), f32),
    }


def reference(x_prompt, x_sample, c_prompt, c_sample, cache_k, cache_v, state_conv, page_table,
              norm1_g, norm2_g, w_ada, b_ada, w_in, q_norm_g, k_norm_g, sb_bias, v_norm_g,
              w_s, b_s, w_out, w_up, conv_w, conv_b, w_down):
    bp, tp, _ = x_prompt.shape
    db, ts, _ = x_sample.shape
    n_pages = page_table.shape[1]
    past_len = n_pages * PAGE_SIZE
    pos_p = jnp.arange(tp, dtype=jnp.int32)
    qpos_s = past_len + jnp.arange(ts, dtype=jnp.int32)
    kpos_s = jnp.arange(past_len + ts, dtype=jnp.int32)
    zero_buf = jnp.zeros((bp, CONV_W - 1, 2 * D_FF), x_prompt.dtype)

    yp, ys = x_prompt, x_sample
    kp_l, vp_l, ks_l, vs_l, gv_l, cp_l, cs_l = [], [], [], [], [], [], []
    for l in range(DEPTH):
        weights = (norm1_g[l], norm2_g[l], w_ada[l], b_ada[l], w_in[l], q_norm_g[l], k_norm_g[l],
                   sb_bias[l], v_norm_g[l], w_s[l], b_s[l], w_out[l], w_up[l], conv_w[l], conv_b[l],
                   w_down[l])
        yp, kp, vp, _, cbp = decoder_layer(yp, c_prompt, None, None, zero_buf, pos_p, pos_p, *weights)
        past_k = cache_k[l][page_table].reshape(db, past_len, H_SB, HEAD_DIM)
        past_v = cache_v[l][page_table].reshape(db, past_len, H_SB, HEAD_DIM)
        ys, ksn, vsn, gvs, cbs = decoder_layer(ys, c_sample, past_k, past_v, state_conv[l],
                                               qpos_s, kpos_s, *weights)
        kp_l.append(kp); vp_l.append(vp); ks_l.append(ksn); vs_l.append(vsn)
        gv_l.append(gvs); cp_l.append(cbp); cs_l.append(cbs)

    k_prompt = jnp.stack(kp_l)
    v_prompt = jnp.stack(vp_l)
    k_sample = jnp.stack(ks_l)
    v_sample = jnp.stack(vs_l)
    gmlp_v_sample = jnp.stack(gv_l)
    conv_prompt = jnp.stack(cp_l)
    conv_sample = jnp.stack(cs_l)
    return (yp, ys, k_prompt, v_prompt, k_sample, v_sample, gmlp_v_sample, conv_prompt, conv_sample)
```

```python
import functools

import jax
import jax.numpy as jnp
from jax import lax
from jax.experimental import pallas as pl
from jax.experimental.pallas import tpu as pltpu

F32 = jnp.float32
BF16 = jnp.bfloat16

EPS = 1e-6
HEAD_DIM = 64
LANES = 128
CHUNK = 128
N_MOD = 6
CONV_W = 3
CARRY_ROWS = 8

VMEM_LIMIT = 56 * 1024 * 1024


def _cparams(sem):
    return pltpu.CompilerParams(dimension_semantics=sem, vmem_limit_bytes=VMEM_LIMIT)


def _ada_kernel(c_ref, w_ref, b_ref, o_ref):
    c = c_ref[...]
    s = c * jax.nn.sigmoid(c)
    o_ref[...] = jnp.dot(s, w_ref[...], preferred_element_type=F32,
                         precision=lax.Precision.HIGHEST) + b_ref[...]


def _ada(c_all, w_ada, b_ada):
    rows, d = c_all.shape
    n = w_ada.shape[1]
    bn = n // 4
    return pl.pallas_call(
        _ada_kernel,
        grid=(n // bn,),
        in_specs=[pl.BlockSpec((rows, d), lambda j: (0, 0)),
                  pl.BlockSpec((d, bn), lambda j: (0, j)),
                  pl.BlockSpec((1, bn), lambda j: (0, j))],
        out_specs=pl.BlockSpec((rows, bn), lambda j: (0, j)),
        out_shape=jax.ShapeDtypeStruct((rows, n), F32),
        compiler_params=_cparams(("arbitrary",)),
        name="ada",
    )(c_all, w_ada, b_ada.reshape(1, n))


def _resident(shape):
    nd = len(shape)
    return pl.BlockSpec(shape, lambda *_: (0,) * nd, pipeline_mode=pl.Buffered(1))


def _mod_spec(per_row, k, tm, d, tiles_per_batch):
    if per_row:
        return pl.BlockSpec((None, tm, d), lambda i: (k, i, 0))
    return pl.BlockSpec((None, None, 1, d), lambda i: (i // tiles_per_batch, k, 0, 0))


def _two_head_norm(blk, g):
    lo = lax.broadcasted_iota(jnp.int32, blk.shape, 1) < HEAD_DIM
    sq = blk * blk
    s_lo = jnp.sum(jnp.where(lo, sq, 0.0), axis=-1, keepdims=True)
    s_hi = jnp.sum(jnp.where(lo, 0.0, sq), axis=-1, keepdims=True)
    r = lax.rsqrt(jnp.where(lo, s_lo, s_hi) * (1.0 / HEAD_DIM) + EPS)
    return blk * r * g


def _inproj_kernel(x_ref, sh_ref, sc_ref, g1_ref, win_ref, gq_ref, gk_ref, gv_ref,
                   qaux_ref, kaux_ref, wmix_ref, bmix_ref, *outs,
                   tm, d_sb, d_mlp, seq_rows, head_major):
    n_pair = d_sb // LANES
    n_grp = d_mlp // CHUNK
    x = x_ref[...]
    ms = jnp.mean(x * x, axis=-1, keepdims=True)
    h = x * lax.rsqrt(ms + EPS) * g1_ref[...]
    h = h * (1.0 + sc_ref[...]) + sh_ref[...]
    proj = jnp.dot(h.astype(BF16), win_ref[...], preferred_element_type=F32)

    if head_major:
        k_ref, v_ref, qh_ref, kh_ref, vb_ref, om_ref = outs
    else:
        k_ref, v_ref, qn_ref, gvo_ref, om_ref = outs

    lo = lax.broadcasted_iota(jnp.int32, (tm, LANES), 1) < HEAD_DIM
    for p in range(n_pair):
        cs = slice(p * LANES, (p + 1) * LANES)
        qn = _two_head_norm(proj[:, p * LANES:(p + 1) * LANES], gq_ref[:, cs])
        kn = _two_head_norm(proj[:, d_sb + p * LANES:d_sb + (p + 1) * LANES], gk_ref[:, cs])
        vv = proj[:, 2 * d_sb + p * LANES:2 * d_sb + (p + 1) * LANES]
        k_ref[:, cs] = kn
        v_ref[:, cs] = vv
        if head_major:
            vb_ref[:, cs] = vv.astype(BF16)
            qh_ref[2 * p] = jnp.where(lo, qn, qaux_ref[2 * p:2 * p + 1, :]).astype(BF16)
            kh_ref[2 * p] = jnp.where(lo, kn, kaux_ref[...]).astype(BF16)
            qr = pltpu.roll(qn, HEAD_DIM, axis=1)
            kr = pltpu.roll(kn, HEAD_DIM, axis=1)
            qh_ref[2 * p + 1] = jnp.where(lo, qr, qaux_ref[2 * p + 1:2 * p + 2, :]).astype(BF16)
            kh_ref[2 * p + 1] = jnp.where(lo, kr, kaux_ref[...]).astype(BF16)
        else:
            qn_ref[:, cs] = qn

    r_i = lax.broadcasted_iota(jnp.int32, (CHUNK, CHUNK), 0)
    c_i = lax.broadcasted_iota(jnp.int32, (CHUNK, CHUNK), 1)
    if seq_rows >= CHUNK:
        keep = r_i >= c_i
    else:
        keep = jnp.logical_and(r_i >= c_i, c_i >= (r_i // seq_rows) * seq_rows)
    u0 = 3 * d_sb
    g0 = 3 * d_sb + d_mlp
    for g in range(n_grp):
        cs = slice(g * CHUNK, (g + 1) * CHUNK)
        wm = jnp.where(keep, wmix_ref[g], 0.0).astype(BF16)
        u = jax.nn.gelu(proj[:, u0 + g * CHUNK:u0 + (g + 1) * CHUNK])
        vg = jax.nn.gelu(proj[:, g0 + g * CHUNK:g0 + (g + 1) * CHUNK])
        vn = vg * lax.rsqrt(jnp.mean(vg * vg, axis=-1, keepdims=True) + EPS) * gv_ref[:, cs]
        if not head_major:
            gvo_ref[:, cs] = vn
        vnb = vn.astype(BF16)
        for c in range(tm // CHUNK):
            rs = slice(c * CHUNK, (c + 1) * CHUNK)
            mixed = jnp.dot(wm, vnb[rs], preferred_element_type=F32) + bmix_ref[:, cs]
            om_ref[rs, cs] = (u[rs] * mixed).astype(BF16)


def _inproj(x2, mod, per_row, n_batch, seq_rows, tm, win_b, g1, gq, gk, gv, qaux, kaux,
            wmix, bmix, d_sb, d_mlp, head_major):
    rows, d = x2.shape
    d_in = win_b.shape[1]
    n_heads = d_sb // HEAD_DIM
    tiles = rows // tm
    tpb = max(tiles // n_batch, 1)
    const = lambda i: (0, 0)
    in_specs = [
        pl.BlockSpec((tm, d), lambda i: (i, 0)),
        _mod_spec(per_row, 0, tm, d, tpb),
        _mod_spec(per_row, 1, tm, d, tpb),
        pl.BlockSpec((1, d), const),
        _resident((d, d_in)),
        pl.BlockSpec((1, d_sb), const),
        pl.BlockSpec((1, d_sb), const),
        pl.BlockSpec((1, d_mlp), const),
        pl.BlockSpec((n_heads, LANES), const),
        pl.BlockSpec((1, LANES), const),
        pl.BlockSpec(wmix.shape, lambda i: (0, 0, 0)),
        pl.BlockSpec((CHUNK, d_mlp), const),
    ]
    row_spec = lambda w: pl.BlockSpec((tm, w), lambda i: (i, 0))
    if head_major:
        t_len = rows // n_batch
        hm_spec = pl.BlockSpec((None, n_heads, tm, LANES), lambda i: (i // tpb, 0, i % tpb, 0))
        out_specs = [row_spec(d_sb), row_spec(d_sb), hm_spec, hm_spec, row_spec(d_sb), row_spec(d_mlp)]
        out_shape = [jax.ShapeDtypeStruct((rows, d_sb), F32),
                     jax.ShapeDtypeStruct((rows, d_sb), F32),
                     jax.ShapeDtypeStruct((n_batch, n_heads, t_len, LANES), BF16),
                     jax.ShapeDtypeStruct((n_batch, n_heads, t_len, LANES), BF16),
                     jax.ShapeDtypeStruct((rows, d_sb), BF16),
                     jax.ShapeDtypeStruct((rows, d_mlp), BF16)]
    else:
        out_specs = [row_spec(d_sb), row_spec(d_sb), row_spec(d_sb), row_spec(d_mlp), row_spec(d_mlp)]
        out_shape = [jax.ShapeDtypeStruct((rows, d_sb), F32),
                     jax.ShapeDtypeStruct((rows, d_sb), F32),
                     jax.ShapeDtypeStruct((rows, d_sb), F32),
                     jax.ShapeDtypeStruct((rows, d_mlp), F32),
                     jax.ShapeDtypeStruct((rows, d_mlp), BF16)]
    kern = functools.partial(_inproj_kernel, tm=tm, d_sb=d_sb, d_mlp=d_mlp,
                             seq_rows=seq_rows, head_major=head_major)
    return pl.pallas_call(
        kern, grid=(tiles,), in_specs=in_specs, out_specs=out_specs, out_shape=out_shape,
        compiler_params=_cparams(("arbitrary",)),
        name="inproj_prompt" if head_major else "inproj_sample",
    )(x2, mod, mod, g1, win_b, gq, gk, gv, qaux, kaux, wmix, bmix)


def _softplus(z):
    return jnp.maximum(z, 0.0) + jnp.log1p(jnp.exp(-jnp.abs(z)))


def _strict_upper(n):
    r = lax.broadcasted_iota(jnp.int32, (n, n), 0)
    c = lax.broadcasted_iota(jnp.int32, (n, n), 1)
    return jnp.where(r > c, 1.0, 0.0).astype(BF16)


def _attn_prompt_kernel(q_ref, k_ref, v_ref, o_ref, acc_ref, car_ref, *, blk):
    i = pl.program_id(2)
    upper = _strict_upper(blk)
    reps = blk // LANES
    row = lax.broadcasted_iota(jnp.int32, (blk, blk), 0)
    col = lax.broadcasted_iota(jnp.int32, (blk, blk), 1)
    causal = col < row

    for hh in range(2):
        q = q_ref[hh]

        def tile(j, mask):
            ks = pl.multiple_of(j * blk, blk)
            k = k_ref[hh, pl.ds(ks, blk), :]
            v = v_ref[pl.ds(ks, blk), :]
            z = lax.dot_general(q, k, (((1,), (1,)), ((), ())), preferred_element_type=F32)
            sp = _softplus(z)
            if mask is not None:
                sp = jnp.where(mask, sp, 0.0)
            later = jnp.dot(sp.astype(BF16), upper, preferred_element_type=F32)
            car = car_ref[...]
            t = z - sp - later - jnp.tile(car, (1, reps))
            a = jnp.exp(t)
            if mask is not None:
                a = jnp.where(mask, a, 0.0)
            acc_ref[hh] += jnp.dot(a.astype(BF16), v, preferred_element_type=F32)
            car_ref[...] = car + jnp.sum(sp, axis=-1, keepdims=True)

        acc_ref[hh] = jnp.zeros((blk, LANES), F32)
        car_ref[...] = jnp.zeros((blk, LANES), F32)
        tile(i, causal)

        def body(n, c):
            tile(i - 1 - n, None)
            return c

        lax.fori_loop(0, i, body, 0)

    lo = lax.broadcasted_iota(jnp.int32, (blk, LANES), 1) < HEAD_DIM
    o_ref[...] = jnp.where(lo, acc_ref[0], acc_ref[1]).astype(o_ref.dtype)


def _attn_prompt(qh, kh, vb, blk):
    n_batch, n_heads, t_len, _ = qh.shape
    d_sb = n_heads * HEAD_DIM
    n_pair = n_heads // 2
    return pl.pallas_call(
        functools.partial(_attn_prompt_kernel, blk=blk),
        grid=(n_batch, n_pair, t_len // blk),
        in_specs=[pl.BlockSpec((None, 2, blk, LANES), lambda b, p, i: (b, p, i, 0)),
                  pl.BlockSpec((None, 2, t_len, LANES), lambda b, p, i: (b, p, 0, 0)),
                  pl.BlockSpec((None, t_len, LANES), lambda b, p, i: (b, 0, p))],
        out_specs=pl.BlockSpec((None, blk, LANES), lambda b, p, i: (b, i, p)),
        out_shape=jax.ShapeDtypeStruct((n_batch, t_len, d_sb), BF16),
        scratch_shapes=[pltpu.VMEM((2, blk, LANES), F32), pltpu.VMEM((blk, LANES), F32)],
        compiler_params=_cparams(("arbitrary", "arbitrary", "arbitrary")),
        name="attn_prompt",
    )(qh, kh, vb)


def _attn_sample_kernel(pt_ref, q_ref, kn_ref, vn_ref, bias_ref, ck_hbm, cv_hbm, o_ref,
                        kbuf, vbuf, sem, acc_ref, car_ref,
                        *, n_tok, n_heads, pages_per_chunk, n_chunks, page, sub):
    b = pl.program_id(0)
    nb = pl.num_programs(0)
    d_sb = n_heads * HEAD_DIM
    rows = n_tok * n_heads
    ck_rows = pages_per_chunk * page

    def chunk_copies(seq, chunk, slot):
        cps = []
        for pg in range(pages_per_chunk):
            pid = pt_ref[seq, chunk * pages_per_chunk + pg]
            dst = pl.ds(pg * page, page)
            cps.append(pltpu.make_async_copy(ck_hbm.at[pid], kbuf.at[slot, dst], sem.at[0, slot]))
            cps.append(pltpu.make_async_copy(cv_hbm.at[pid], vbuf.at[slot, dst], sem.at[1, slot]))
        return cps

    @pl.when(b == 0)
    def _():
        for cp in chunk_copies(0, n_chunks - 1, (n_chunks - 1) % 2):
            cp.start()

    r_i = lax.broadcasted_iota(jnp.int32, (rows, d_sb), 0)
    c_i = lax.broadcasted_iota(jnp.int32, (rows, d_sb), 1)
    own = (r_i % n_heads) == (c_i // HEAD_DIM)
    q_nat = q_ref[...]
    q_rep = jnp.concatenate(
        [jnp.broadcast_to(q_nat[t:t + 1, :], (n_heads, d_sb)) for t in range(n_tok)], axis=0)
    q_bd = jnp.where(own, q_rep, 0.0)
    q_bdb = q_bd.astype(BF16)
    bias = bias_ref[...]
    tok = lax.broadcasted_iota(jnp.int32, (rows, 1), 0) // n_heads

    acc = jnp.zeros((rows, d_sb), F32)
    car = jnp.zeros((rows, 1), F32)
    k_new = kn_ref[...]
    v_new = vn_ref[...]
    for s in range(n_tok - 1, -1, -1):
        z = jnp.sum(q_bd * k_new[s:s + 1, :], axis=-1, keepdims=True) + bias
        vis = tok > s
        sp = jnp.where(vis, _softplus(z), 0.0)
        a = jnp.where(vis, jnp.exp(z - sp - car), 0.0)
        acc = acc + a * v_new[s:s + 1, :]
        car = car + sp
    acc_ref[...] = acc
    car_ref[...] = jnp.broadcast_to(car, (rows, LANES))

    upper = _strict_upper(sub)

    def chunk_body(n, carry):
        chunk = n_chunks - 1 - n
        slot = chunk % 2
        for cp in chunk_copies(b, chunk, slot):
            cp.wait()

        @pl.when(chunk > 0)
        def _():
            for cp in chunk_copies(b, chunk - 1, 1 - slot):
                cp.start()

        @pl.when(jnp.logical_and(chunk == 0, b + 1 < nb))
        def _():
            for cp in chunk_copies(b + 1, n_chunks - 1, 1 - slot):
                cp.start()

        kc = kbuf[slot].astype(BF16)
        z_all = lax.dot_general(q_bdb, kc, (((1,), (1,)), ((), ())),
                                preferred_element_type=F32) + bias
        for sb in range(ck_rows // sub - 1, -1, -1):
            z = z_all[:, sb * sub:(sb + 1) * sub]
            sp = _softplus(z)
            later = jnp.dot(sp.astype(BF16), upper, preferred_element_type=F32)
            cr = car_ref[...]
            t = z - sp - later - jnp.tile(cr, (1, sub // LANES))
            a = jnp.exp(t).astype(BF16)
            vc = vbuf[slot, pl.ds(sb * sub, sub), :].astype(BF16)
            acc_ref[...] += jnp.dot(a, vc, preferred_element_type=F32)
            car_ref[...] = cr + jnp.sum(sp, axis=-1, keepdims=True)
        return carry

    lax.fori_loop(0, n_chunks, chunk_body, 0)

    res = jnp.where(own, acc_ref[...], 0.0)
    o_ref[...] = jnp.concatenate(
        [jnp.sum(res[t * n_heads:(t + 1) * n_heads], axis=0, keepdims=True)
         for t in range(n_tok)], axis=0)


def _attn_sample(page_table, q_nat, k_new, v_new, bias_rows, cache_k, cache_v, n_heads):
    n_seq, n_tok, d_sb = q_nat.shape
    n_pages = page_table.shape[1]
    page = cache_k.shape[1]
    pages_per_chunk = 8
    n_chunks = n_pages // pages_per_chunk
    assert n_chunks * pages_per_chunk == n_pages and n_chunks % 2 == 0
    rows = n_tok * n_heads
    ck_rows = pages_per_chunk * page
    tok_spec = pl.BlockSpec((None, n_tok, d_sb), lambda b, pt: (b, 0, 0))
    grid_spec = pltpu.PrefetchScalarGridSpec(
        num_scalar_prefetch=1,
        grid=(n_seq,),
        in_specs=[tok_spec, tok_spec, tok_spec,
                  pl.BlockSpec((rows, 1), lambda b, pt: (0, 0)),
                  pl.BlockSpec(memory_space=pl.ANY),
                  pl.BlockSpec(memory_space=pl.ANY)],
        out_specs=pl.BlockSpec((None, n_tok, d_sb), lambda b, pt: (b, 0, 0)),
        scratch_shapes=[pltpu.VMEM((2, ck_rows, d_sb), F32),
                        pltpu.VMEM((2, ck_rows, d_sb), F32),
                        pltpu.SemaphoreType.DMA((2, 2)),
                        pltpu.VMEM((rows, d_sb), F32),
                        pltpu.VMEM((rows, LANES), F32)],
    )
    kern = functools.partial(_attn_sample_kernel, n_tok=n_tok, n_heads=n_heads,
                             pages_per_chunk=pages_per_chunk, n_chunks=n_chunks,
                             page=page, sub=256)
    return pl.pallas_call(
        kern, grid_spec=grid_spec,
        out_shape=jax.ShapeDtypeStruct((n_seq, n_tok, d_sb), F32),
        compiler_params=_cparams(("arbitrary",)),
        name="attn_sample",
    )(page_table, q_nat, k_new, v_new, bias_rows, cache_k, cache_v)


def _outffn_kernel(*refs, tm, d_ff, fc, seq_rows, tiles_per_batch, has_state):
    if has_state:
        (x_ref, osb_ref, oml_ref, g1_ref, sh_ref, sc_ref, g2_ref, n2_ref, wout_ref, wup_ref,
         cw_ref, cb_ref, wdn_ref, p1_ref, p2_ref, y_ref, up_ref) = refs
    else:
        (x_ref, osb_ref, oml_ref, g1_ref, sh_ref, sc_ref, g2_ref, n2_ref, wout_ref, wup_ref,
         cw_ref, cb_ref, wdn_ref, y_ref, st_ref, carry_ref) = refs
    d_sb = osb_ref.shape[1]
    mix = (jnp.dot(osb_ref[...].astype(BF16), wout_ref[0:d_sb, :], preferred_element_type=F32)
           + jnp.dot(oml_ref[...], wout_ref[d_sb:, :], preferred_element_type=F32))
    x1 = x_ref[...] + g1_ref[...] * mix
    ms = jnp.mean(x1 * x1, axis=-1, keepdims=True)
    h2 = x1 * lax.rsqrt(ms + EPS) * n2_ref[...]
    h2 = (h2 * (1.0 + sc_ref[...]) + sh_ref[...]).astype(BF16)

    row = lax.broadcasted_iota(jnp.int32, (tm, 1), 0)
    if has_state:
        t_in = row % seq_rows
    else:
        t_in = row
        first = (pl.program_id(0) % tiles_per_batch) == 0

        @pl.when(first)
        def _():
            carry_ref[...] = jnp.zeros_like(carry_ref)

    def conv(u, cols):
        r1 = pltpu.roll(u, 1, axis=0)
        r2 = pltpu.roll(u, 2, axis=0)
        if has_state:
            up_ref[:, cols] = u
            prev1 = jnp.where(t_in >= 1, r1, p1_ref[:, cols])
            prev2 = jnp.where(t_in >= 2, r2, p2_ref[:, cols])
        else:
            c6 = carry_ref[CARRY_ROWS - 2:CARRY_ROWS - 1, cols]
            c7 = carry_ref[CARRY_ROWS - 1:CARRY_ROWS, cols]
            prev1 = jnp.where(t_in >= 1, r1, c7)
            prev2 = jnp.where(t_in >= 2, r2, jnp.where(t_in == 0, c6, c7))
            carry_ref[:, cols] = u[tm - CARRY_ROWS:, :]
        return (cb_ref[:, cols] + cw_ref[0:1, cols] * prev2 + cw_ref[1:2, cols] * prev1
                + cw_ref[2:3, cols] * u)

    acc = jnp.zeros((tm, x1.shape[1]), F32)
    for f in range(d_ff // fc):
        gc = slice(f * fc, (f + 1) * fc)
        vc = slice(d_ff + f * fc, d_ff + (f + 1) * fc)
        cg = conv(jnp.dot(h2, wup_ref[:, gc], preferred_element_type=F32), gc)
        cv = conv(jnp.dot(h2, wup_ref[:, vc], preferred_element_type=F32), vc)
        act = (cg * jax.nn.sigmoid(cg) * cv).astype(BF16)
        acc = acc + jnp.dot(act, wdn_ref[gc, :], preferred_element_type=F32)
    y_ref[...] = x1 + g2_ref[...] * acc

    if not has_state:
        @pl.when((pl.program_id(0) % tiles_per_batch) == tiles_per_batch - 1)
        def _():
            st_ref[...] = carry_ref[...]


def _outffn(x2, osb, oml, mod, per_row, n_batch, seq_rows, tm, n2, wout_b, wup_b, conv_w, conv_b,
            wdn_b, state_rows):
    rows, d = x2.shape
    d_ff = wdn_b.shape[0]
    d_sb = osb.shape[1]
    d_mlp = oml.shape[1]
    tiles = rows // tm
    tpb = max(tiles // n_batch, 1)
    has_state = state_rows is not None
    const = lambda i: (0, 0)
    row_spec = lambda w: pl.BlockSpec((tm, w), lambda i: (i, 0))
    in_specs = [
        row_spec(d), row_spec(d_sb), row_spec(d_mlp),
        _mod_spec(per_row, 2, tm, d, tpb),
        _mod_spec(per_row, 3, tm, d, tpb),
        _mod_spec(per_row, 4, tm, d, tpb),
        _mod_spec(per_row, 5, tm, d, tpb),
        pl.BlockSpec((1, d), const),
        _resident(wout_b.shape),
        _resident(wup_b.shape),
        pl.BlockSpec(conv_w.shape, const),
        pl.BlockSpec((1, 2 * d_ff), const),
        _resident(wdn_b.shape),
    ]
    args = [x2, osb, oml, mod, mod, mod, mod, n2, wout_b, wup_b, conv_w, conv_b, wdn_b]
    if has_state:
        in_specs += [_resident((rows, 2 * d_ff)), _resident((rows, 2 * d_ff))]
        args += list(state_rows)
        out_specs = [row_spec(d), row_spec(2 * d_ff)]
        out_shape = [jax.ShapeDtypeStruct((rows, d), F32),
                     jax.ShapeDtypeStruct((rows, 2 * d_ff), F32)]
        scratch = []
    else:
        out_specs = [row_spec(d),
                     pl.BlockSpec((None, CARRY_ROWS, 2 * d_ff), lambda i: (i // tpb, 0, 0))]
        out_shape = [jax.ShapeDtypeStruct((rows, d), F32),
                     jax.ShapeDtypeStruct((n_batch, CARRY_ROWS, 2 * d_ff), F32)]
        scratch = [pltpu.VMEM((CARRY_ROWS, 2 * d_ff), F32)]
    kern = functools.partial(_outffn_kernel, tm=tm, d_ff=d_ff, fc=d_ff // 2, seq_rows=seq_rows,
                             tiles_per_batch=tpb, has_state=has_state)
    return pl.pallas_call(
        kern, grid=(tiles,), in_specs=in_specs, out_specs=out_specs, out_shape=out_shape,
        scratch_shapes=scratch,
        compiler_params=_cparams(("arbitrary",)),
        name="outffn_sample" if has_state else "outffn_prompt",
    )(*args)


def _layer(l, yp, ys, c_prompt, c_sample, cache_k, cache_v, state_conv, page_table,
           norm1_g, norm2_g, w_ada, b_ada, w_in, q_norm_g, k_norm_g, sb_bias, v_norm_g,
           w_s, b_s, w_out, w_up, conv_w, conv_b, w_down):
    bp, tp, d = yp.shape
    db, ts, _ = ys.shape
    d_ff = w_down.shape[1]
    n_heads = sb_bias.shape[1]
    d_sb = n_heads * HEAD_DIM
    n_grp = v_norm_g.shape[1]
    d_mlp = n_grp * CHUNK
    n_pool, page = cache_k.shape[1], cache_k.shape[2]

    n_c = bp + db
    pad = (-n_c) % 8
    c_all = jnp.concatenate([c_prompt, c_sample, jnp.zeros((pad, d), F32)], axis=0)
    mod = _ada(c_all, w_ada[l], b_ada[l])
    mod_p = mod[:bp].reshape(bp, N_MOD, 1, d)
    mod_s = jnp.repeat(mod[bp:n_c].reshape(db, N_MOD, d).transpose(1, 0, 2), ts, axis=1)

    win_b = w_in[l].astype(BF16)
    wout_b = w_out[l].astype(BF16)
    wup_b = w_up[l].astype(BF16)
    wdn_b = w_down[l].astype(BF16)
    g1 = norm1_g[l].reshape(1, d)
    n2 = norm2_g[l].reshape(1, d)
    gq = (jnp.tile(q_norm_g[l], n_heads) * (HEAD_DIM ** -0.5)).reshape(1, d_sb)
    gk = jnp.tile(k_norm_g[l], n_heads).reshape(1, d_sb)
    gv = v_norm_g[l].reshape(1, d_mlp)
    aux_lane = jnp.arange(LANES) == HEAD_DIM
    qaux = jnp.where(aux_lane[None, :], sb_bias[l][:, None], 0.0).astype(F32)
    kaux = aux_lane.astype(F32).reshape(1, LANES)
    cb = conv_b[l].reshape(1, 2 * d_ff)

    wmix_p = w_s[l]
    bmix_p = jnp.repeat(b_s[l].T, CHUNK, axis=1)
    xp2 = yp.reshape(bp * tp, d)
    kp, vp, qh, kh, vb, oml_p = _inproj(
        xp2, mod_p, False, bp, tp, 256, win_b, g1, gq, gk, gv, qaux, kaux, wmix_p, bmix_p,
        d_sb, d_mlp, True)
    osb_p = _attn_prompt(qh, kh, vb.reshape(bp, tp, d_sb), 256).reshape(bp * tp, d_sb)
    yp2, st_p = _outffn(xp2, osb_p, oml_p, mod_p, False, bp, tp, 256, n2, wout_b, wup_b,
                        conv_w[l], cb, wdn_b, None)

    rows_s = db * ts
    wmix_s = jnp.tile(w_s[l][:, :ts, :ts], (1, CHUNK // ts, CHUNK // ts))
    bmix_s = jnp.repeat(jnp.tile(b_s[l][:, :ts], (1, CHUNK // ts)).T, CHUNK, axis=1)
    xs2 = ys.reshape(rows_s, d)
    ks_, vs_, qn_s, gv_s, oml_s = _inproj(
        xs2, mod_s, True, db, ts, rows_s, win_b, g1, gq, gk, gv, qaux, kaux, wmix_s, bmix_s,
        d_sb, d_mlp, False)
    bias_rows = jnp.tile(sb_bias[l], ts).reshape(ts * n_heads, 1)
    osb_s = _attn_sample(page_table, qn_s.reshape(db, ts, d_sb), ks_.reshape(db, ts, d_sb),
                         vs_.reshape(db, ts, d_sb), bias_rows,
                         cache_k[l].reshape(n_pool, page, d_sb),
                         cache_v[l].reshape(n_pool, page, d_sb), n_heads)
    st = state_conv[l]
    zero = jnp.zeros((db, ts - 2, 2 * d_ff), F32)
    p1 = jnp.concatenate([st[:, 1:2], jnp.zeros((db, ts - 1, 2 * d_ff), F32)], axis=1)
    p2 = jnp.concatenate([st, zero], axis=1)
    ys2, up_s = _outffn(xs2, osb_s.reshape(rows_s, d_sb), oml_s, mod_s, True, db, ts, rows_s, n2,
                        wout_b, wup_b, conv_w[l], cb, wdn_b,
                        (p1.reshape(rows_s, 2 * d_ff), p2.reshape(rows_s, 2 * d_ff)))

    outs = (kp.reshape(bp, tp, n_heads, HEAD_DIM), vp.reshape(bp, tp, n_heads, HEAD_DIM),
            ks_.reshape(db, ts, n_heads, HEAD_DIM), vs_.reshape(db, ts, n_heads, HEAD_DIM),
            gv_s.reshape(db, ts, n_grp, CHUNK),
            st_p[:, CARRY_ROWS - (CONV_W - 1):],
            up_s.reshape(db, ts, 2 * d_ff)[:, ts - (CONV_W - 1):])
    return yp2.reshape(bp, tp, d), ys2.reshape(db, ts, d), outs


def kernel(x_prompt, x_sample, c_prompt, c_sample, cache_k, cache_v, state_conv, page_table,
           norm1_g, norm2_g, w_ada, b_ada, w_in, q_norm_g, k_norm_g, sb_bias, v_norm_g,
           w_s, b_s, w_out, w_up, conv_w, conv_b, w_down):
    depth = w_in.shape[0]
    yp, ys = x_prompt, x_sample
    per_layer = []
    for l in range(depth):
        yp, ys, outs = _layer(l, yp, ys, c_prompt, c_sample, cache_k, cache_v, state_conv,
                              page_table, norm1_g, norm2_g, w_ada, b_ada, w_in, q_norm_g,
                              k_norm_g, sb_bias, v_norm_g, w_s, b_s, w_out, w_up, conv_w,
                              conv_b, w_down)
        per_layer.append(outs)
    stacked = tuple(jnp.stack([o[i] for o in per_layer]) for i in range(7))
    return (yp, ys) + stacked
```

```python
import functools

import jax
import jax.numpy as jnp
from jax import lax
from jax.experimental import pallas as pl
from jax.experimental.pallas import tpu as pltpu

F32 = jnp.float32
BF16 = jnp.bfloat16

EPS = 1e-6
LOG2E = 1.4426950408889634
HEAD_DIM = 64
LANES = 128
CHUNK = 128
N_MOD = 6
CONV_W = 3
CARRY_ROWS = 8
ATTN_BLK = 256
HEADS_PER_STEP = 8

VMEM_LIMIT = 56 * 1024 * 1024

NT_DIMS = (((1,), (1,)), ((), ()))


def _cparams(sem):
    return pltpu.CompilerParams(dimension_semantics=sem, vmem_limit_bytes=VMEM_LIMIT)


def _resident(shape):
    nd = len(shape)
    return pl.BlockSpec(shape, lambda *_: (0,) * nd, pipeline_mode=pl.Buffered(1))


def _mod_spec(per_row, k, tm, d, tiles_per_batch):
    if per_row:
        return pl.BlockSpec((None, tm, d), lambda i: (k, i, 0))
    return pl.BlockSpec((None, None, 1, d), lambda i: (i // tiles_per_batch, k, 0, 0))


def _ada_kernel(c_ref, w_ref, b_ref, o_ref):
    c = c_ref[...]
    s = c * jax.nn.sigmoid(c)
    o_ref[...] = jnp.dot(s, w_ref[...], preferred_element_type=F32,
                         precision=lax.Precision.HIGHEST) + b_ref[...]


def _ada(c_all, w_ada, b_ada):
    rows, d = c_all.shape
    n = w_ada.shape[1]
    bn = n // 4
    return pl.pallas_call(
        _ada_kernel,
        grid=(n // bn,),
        in_specs=[pl.BlockSpec((rows, d), lambda j: (0, 0)),
                  pl.BlockSpec((d, bn), lambda j: (0, j)),
                  pl.BlockSpec((1, bn), lambda j: (0, j))],
        out_specs=pl.BlockSpec((rows, bn), lambda j: (0, j)),
        out_shape=jax.ShapeDtypeStruct((rows, n), F32),
        compiler_params=_cparams(("arbitrary",)),
        name="ada",
    )(c_all, w_ada, b_ada.reshape(1, n))


def _two_head_norm(blk, g):
    lo = lax.broadcasted_iota(jnp.int32, blk.shape, 1) < HEAD_DIM
    sq = blk * blk
    s_lo = jnp.sum(jnp.where(lo, sq, 0.0), axis=-1, keepdims=True)
    s_hi = jnp.sum(jnp.where(lo, 0.0, sq), axis=-1, keepdims=True)
    r = lax.rsqrt(jnp.where(lo, s_lo, s_hi) * (1.0 / HEAD_DIM) + EPS)
    return blk * r * g


def _spatial_gate(proj, u0, g0, gv_ref, wmix_ref, bmix_ref, om_ref, gvo_ref, tm, n_grp, seq_rows):
    r_i = lax.broadcasted_iota(jnp.int32, (CHUNK, CHUNK), 0)
    c_i = lax.broadcasted_iota(jnp.int32, (CHUNK, CHUNK), 1)
    if seq_rows >= CHUNK:
        keep = r_i >= c_i
    else:
        keep = jnp.logical_and(r_i >= c_i, c_i >= (r_i // seq_rows) * seq_rows)
    for g in range(n_grp):
        cs = slice(g * CHUNK, (g + 1) * CHUNK)
        wm = jnp.where(keep, wmix_ref[g], 0.0).astype(BF16)
        u = jax.nn.gelu(proj[:, u0 + g * CHUNK:u0 + (g + 1) * CHUNK])
        vg = jax.nn.gelu(proj[:, g0 + g * CHUNK:g0 + (g + 1) * CHUNK])
        vn = vg * lax.rsqrt(jnp.mean(vg * vg, axis=-1, keepdims=True) + EPS) * gv_ref[:, cs]
        if gvo_ref is not None:
            gvo_ref[:, cs] = vn
        vnb = vn.astype(BF16)
        for c in range(tm // CHUNK):
            rs = slice(c * CHUNK, (c + 1) * CHUNK)
            mixed = jnp.dot(wm, vnb[rs], preferred_element_type=F32) + bmix_ref[:, cs]
            om_ref[rs, cs] = (u[rs] * mixed).astype(BF16)


def _normed_input(x_ref, sh_ref, sc_ref, g_ref):
    x = x_ref[...]
    ms = jnp.mean(x * x, axis=-1, keepdims=True)
    h = x * lax.rsqrt(ms + EPS) * g_ref[...]
    return (h * (1.0 + sc_ref[...]) + sh_ref[...]).astype(BF16)


def _inproj_prompt_kernel(x_ref, sh_ref, sc_ref, g1_ref, wnat_ref, wkvt_ref, gq_ref, gkc_ref,
                          gv_ref, qaux_ref, wmix_ref, bmix_ref,
                          kt_ref, vt_ref, qh_ref, kh_ref, vb_ref, om_ref,
                          *, tm, d_sb, d_mlp, seq_rows):
    n_heads = d_sb // HEAD_DIM
    hb = _normed_input(x_ref, sh_ref, sc_ref, g1_ref)
    proj = jnp.dot(hb, wnat_ref[...], preferred_element_type=F32)
    kvt = lax.dot_general(wkvt_ref[...], hb, NT_DIMS, preferred_element_type=F32)

    gkc = jnp.tile(gkc_ref[...], (1, tm // LANES))
    aux_rows = jnp.where(lax.broadcasted_iota(jnp.int32, (HEAD_DIM, tm), 0) == 0, 1.0, 0.0)
    for hd in range(n_heads):
        rs = slice(hd * HEAD_DIM, (hd + 1) * HEAD_DIM)
        blk = kvt[rs, :]
        kn = blk * lax.rsqrt(jnp.mean(blk * blk, axis=0, keepdims=True) + EPS) * gkc
        kt_ref[rs, :] = kn
        kh_ref[hd, 0:HEAD_DIM, :] = kn.astype(BF16)
        kh_ref[hd, HEAD_DIM:, :] = aux_rows.astype(BF16)
    vt_ref[...] = kvt[d_sb:, :]

    lo = lax.broadcasted_iota(jnp.int32, (tm, LANES), 1) < HEAD_DIM
    for p in range(d_sb // LANES):
        cs = slice(p * LANES, (p + 1) * LANES)
        qn = _two_head_norm(proj[:, p * LANES:(p + 1) * LANES], gq_ref[:, cs])
        vb_ref[:, cs] = proj[:, d_sb + p * LANES:d_sb + (p + 1) * LANES].astype(BF16)
        qh_ref[2 * p] = jnp.where(lo, qn, qaux_ref[2 * p:2 * p + 1, :]).astype(BF16)
        qr = pltpu.roll(qn, HEAD_DIM, axis=1)
        qh_ref[2 * p + 1] = jnp.where(lo, qr, qaux_ref[2 * p + 1:2 * p + 2, :]).astype(BF16)

    _spatial_gate(proj, 2 * d_sb, 2 * d_sb + d_mlp, gv_ref, wmix_ref, bmix_ref, om_ref, None,
                  tm, d_mlp // CHUNK, seq_rows)


def _inproj_sample_kernel(x_ref, sh_ref, sc_ref, g1_ref, win_ref, gq_ref, gk_ref, gv_ref,
                          wmix_ref, bmix_ref, k_ref, v_ref, qn_ref, gvo_ref, om_ref,
                          *, tm, d_sb, d_mlp, seq_rows):
    hb = _normed_input(x_ref, sh_ref, sc_ref, g1_ref)
    proj = jnp.dot(hb, win_ref[...], preferred_element_type=F32)
    for p in range(d_sb // LANES):
        cs = slice(p * LANES, (p + 1) * LANES)
        qn_ref[:, cs] = _two_head_norm(proj[:, p * LANES:(p + 1) * LANES], gq_ref[:, cs])
        k_ref[:, cs] = _two_head_norm(proj[:, d_sb + p * LANES:d_sb + (p + 1) * LANES],
                                      gk_ref[:, cs])
        v_ref[:, cs] = proj[:, 2 * d_sb + p * LANES:2 * d_sb + (p + 1) * LANES]
    _spatial_gate(proj, 3 * d_sb, 3 * d_sb + d_mlp, gv_ref, wmix_ref, bmix_ref, om_ref, gvo_ref,
                  tm, d_mlp // CHUNK, seq_rows)


def _inproj_prompt(x2, mod, n_batch, tm, wnat_b, wkvt_b, g1, gq, gkc, gv, qaux, wmix, bmix,
                   d_sb, d_mlp):
    rows, d = x2.shape
    n_heads = d_sb // HEAD_DIM
    t_len = rows // n_batch
    tpb = t_len // tm
    const = lambda i: (0, 0)
    in_specs = [
        pl.BlockSpec((tm, d), lambda i: (i, 0)),
        _mod_spec(False, 0, tm, d, tpb),
        _mod_spec(False, 1, tm, d, tpb),
        pl.BlockSpec((1, d), const),
        _resident(wnat_b.shape),
        _resident(wkvt_b.shape),
        pl.BlockSpec((1, d_sb), const),
        pl.BlockSpec((HEAD_DIM, LANES), const),
        pl.BlockSpec((1, d_mlp), const),
        pl.BlockSpec((n_heads, LANES), const),
        pl.BlockSpec(wmix.shape, lambda i: (0, 0, 0)),
        pl.BlockSpec((CHUNK, d_mlp), const),
    ]
    row_spec = lambda w: pl.BlockSpec((tm, w), lambda i: (i, 0))
    tmin_spec = pl.BlockSpec((None, d_sb, tm), lambda i: (i // tpb, 0, i % tpb))
    out_specs = [
        tmin_spec, tmin_spec,
        pl.BlockSpec((None, n_heads, tm, LANES), lambda i: (i // tpb, 0, i % tpb, 0)),
        pl.BlockSpec((None, n_heads, None, LANES, tm), lambda i: (i // tpb, 0, i % tpb, 0, 0)),
        row_spec(d_sb), row_spec(d_mlp)]
    out_shape = [
        jax.ShapeDtypeStruct((n_batch, d_sb, t_len), F32),
        jax.ShapeDtypeStruct((n_batch, d_sb, t_len), F32),
        jax.ShapeDtypeStruct((n_batch, n_heads, t_len, LANES), BF16),
        jax.ShapeDtypeStruct((n_batch, n_heads, tpb, LANES, tm), BF16),
        jax.ShapeDtypeStruct((rows, d_sb), BF16),
        jax.ShapeDtypeStruct((rows, d_mlp), BF16)]
    kern = functools.partial(_inproj_prompt_kernel, tm=tm, d_sb=d_sb, d_mlp=d_mlp, seq_rows=t_len)
    return pl.pallas_call(
        kern, grid=(rows // tm,), in_specs=in_specs, out_specs=out_specs, out_shape=out_shape,
        compiler_params=_cparams(("arbitrary",)), name="inproj_prompt",
    )(x2, mod, mod, g1, wnat_b, wkvt_b, gq, gkc, gv, qaux, wmix, bmix)


def _inproj_sample(x2, mod, seq_rows, win_b, g1, gq, gk, gv, wmix, bmix, d_sb, d_mlp):
    rows, d = x2.shape
    tm = rows
    const = lambda i: (0, 0)
    in_specs = [
        pl.BlockSpec((tm, d), lambda i: (i, 0)),
        _mod_spec(True, 0, tm, d, 1),
        _mod_spec(True, 1, tm, d, 1),
        pl.BlockSpec((1, d), const),
        _resident(win_b.shape),
        pl.BlockSpec((1, d_sb), const),
        pl.BlockSpec((1, d_sb), const),
        pl.BlockSpec((1, d_mlp), const),
        pl.BlockSpec(wmix.shape, lambda i: (0, 0, 0)),
        pl.BlockSpec((CHUNK, d_mlp), const),
    ]
    row_spec = lambda w: pl.BlockSpec((tm, w), lambda i: (i, 0))
    out_specs = [row_spec(d_sb), row_spec(d_sb), row_spec(d_sb), row_spec(d_mlp), row_spec(d_mlp)]
    out_shape = [jax.ShapeDtypeStruct((rows, d_sb), F32),
                 jax.ShapeDtypeStruct((rows, d_sb), F32),
                 jax.ShapeDtypeStruct((rows, d_sb), F32),
                 jax.ShapeDtypeStruct((rows, d_mlp), F32),
                 jax.ShapeDtypeStruct((rows, d_mlp), BF16)]
    kern = functools.partial(_inproj_sample_kernel, tm=tm, d_sb=d_sb, d_mlp=d_mlp,
                             seq_rows=seq_rows)
    return pl.pallas_call(
        kern, grid=(1,), in_specs=in_specs, out_specs=out_specs, out_shape=out_shape,
        compiler_params=_cparams(("arbitrary",)), name="inproj_sample",
    )(x2, mod, mod, g1, win_b, gq, gk, gv, wmix, bmix)


def _softplus2(z2):
    neg_abs = pltpu.bitcast(pltpu.bitcast(z2, jnp.uint32) | jnp.uint32(0x80000000), F32)
    return jnp.maximum(z2, 0.0) + jnp.log(1.0 + jnp.exp2(neg_abs)) * LOG2E


def _strict_upper(n):
    r = lax.broadcasted_iota(jnp.int32, (n, n), 0)
    c = lax.broadcasted_iota(jnp.int32, (n, n), 1)
    return jnp.where(r > c, 1.0, 0.0).astype(BF16)


def _attn_prompt_kernel(q_ref, k_ref, v_ref, o_ref, acc_ref, car_ref, *, blk, n_h):
    i = pl.program_id(2)
    upper = _strict_upper(blk)
    reps = blk // LANES
    row = lax.broadcasted_iota(jnp.int32, (blk, blk), 0)
    col = lax.broadcasted_iota(jnp.int32, (blk, blk), 1)
    causal = col < row

    def tiles(j, mask):
        heads = range(n_h)
        ks = pl.multiple_of(j * blk, blk)
        zs = [jnp.dot(q_ref[hh], k_ref[hh, j], preferred_element_type=F32) for hh in heads]
        sps = [_softplus2(z) for z in zs]
        if mask is not None:
            sps = [jnp.where(mask, sp, 0.0) for sp in sps]
        lbs = [zs[hh] - sps[hh] for hh in heads]
        laters = [jnp.dot(sp.astype(BF16), upper, preferred_element_type=F32) for sp in sps]
        cars = [car_ref[hh] for hh in heads]
        for hh in heads:
            car_ref[hh] = cars[hh] + jnp.sum(sps[hh], axis=-1, keepdims=True)
        a_s = [jnp.exp2(lbs[hh] - laters[hh] - jnp.tile(cars[hh], (1, reps))) for hh in heads]
        if mask is not None:
            a_s = [jnp.where(mask, a, 0.0) for a in a_s]
        for hh in heads:
            v = v_ref[pl.ds(ks, blk), (hh // 2) * LANES:(hh // 2 + 1) * LANES]
            acc_ref[hh] += jnp.dot(a_s[hh].astype(BF16), v, preferred_element_type=F32)

    acc_ref[...] = jnp.zeros_like(acc_ref)
    car_ref[...] = jnp.zeros_like(car_ref)
    tiles(i, causal)

    def body(n, c):
        tiles(i - 1 - n, None)
        return c

    lax.fori_loop(0, i, body, 0)

    lo = lax.broadcasted_iota(jnp.int32, (blk, LANES), 1) < HEAD_DIM
    for p in range(n_h // 2):
        o_ref[:, p * LANES:(p + 1) * LANES] = jnp.where(
            lo, acc_ref[2 * p], acc_ref[2 * p + 1]).astype(o_ref.dtype)


def _attn_prompt(qh, kh, vb):
    n_batch, n_heads, t_len, _ = qh.shape
    blk = kh.shape[-1]
    n_h = HEADS_PER_STEP
    d_sb = n_heads * HEAD_DIM
    w = n_h * HEAD_DIM
    return pl.pallas_call(
        functools.partial(_attn_prompt_kernel, blk=blk, n_h=n_h),
        grid=(n_batch, n_heads // n_h, t_len // blk),
        in_specs=[pl.BlockSpec((None, n_h, blk, LANES), lambda b, g, i: (b, g, i, 0)),
                  pl.BlockSpec((None, n_h, t_len // blk, LANES, blk),
                               lambda b, g, i: (b, g, 0, 0, 0)),
                  pl.BlockSpec((None, t_len, w), lambda b, g, i: (b, 0, g))],
        out_specs=pl.BlockSpec((None, blk, w), lambda b, g, i: (b, i, g)),
        out_shape=jax.ShapeDtypeStruct((n_batch, t_len, d_sb), BF16),
        scratch_shapes=[pltpu.VMEM((n_h, blk, LANES), F32), pltpu.VMEM((n_h, blk, LANES), F32)],
        compiler_params=_cparams(("arbitrary", "arbitrary", "arbitrary")),
        name="attn_prompt",
    )(qh, kh, vb)


def _attn_sample_kernel(pt_ref, q_ref, kn_ref, vn_ref, bias_ref, ck_hbm, cv_hbm, o_ref,
                        kbuf, vbuf, sem, acc_ref, acct_ref, car_ref,
                        *, n_tok, n_heads, pages_per_chunk, n_chunks, page, sub):
    b = pl.program_id(0)
    nb = pl.num_programs(0)
    d_sb = n_heads * HEAD_DIM
    rows = n_tok * n_heads
    ck = pages_per_chunk * page
    n_sub = ck // sub

    def chunk_copies(seq, chunk, slot):
        cps = []
        for pg in range(pages_per_chunk):
            pid = pt_ref[seq, chunk * pages_per_chunk + pg]
            dst = pl.ds(pg * page, page)
            cps.append(pltpu.make_async_copy(ck_hbm.at[pid], kbuf.at[slot, :, dst], sem.at[0, slot]))
            cps.append(pltpu.make_async_copy(cv_hbm.at[pid], vbuf.at[slot, :, dst], sem.at[1, slot]))
        return cps

    @pl.when(b == 0)
    def _():
        for cp in chunk_copies(0, n_chunks - 1, (n_chunks - 1) % 2):
            cp.start()

    r_i = lax.broadcasted_iota(jnp.int32, (rows, d_sb), 0)
    c_i = lax.broadcasted_iota(jnp.int32, (rows, d_sb), 1)
    own = (r_i % n_heads) == (c_i // HEAD_DIM)
    q_nat = q_ref[...]
    q_rep = jnp.concatenate(
        [jnp.broadcast_to(q_nat[t:t + 1, :], (n_heads, d_sb)) for t in range(n_tok)], axis=0)
    q_bd = jnp.where(own, q_rep, 0.0)
    q_bdb = q_bd.astype(BF16)
    bias = bias_ref[...]
    tok = lax.broadcasted_iota(jnp.int32, (rows, 1), 0) // n_heads

    acc = jnp.zeros((rows, d_sb), F32)
    car = jnp.zeros((rows, 1), F32)
    k_new = kn_ref[...]
    v_new = vn_ref[...]
    for s in range(n_tok - 1, -1, -1):
        z = jnp.sum(q_bd * k_new[s:s + 1, :], axis=-1, keepdims=True) + bias
        vis = tok > s
        sp = jnp.where(vis, _softplus2(z), 0.0)
        a = jnp.where(vis, jnp.exp2(z - sp - car), 0.0)
        acc = acc + a * v_new[s:s + 1, :]
        car = car + sp
    acc_ref[...] = acc
    car_ref[...] = jnp.broadcast_to(car, (rows, LANES))

    upper = _strict_upper(sub)

    def chunk_body(n, carry):
        chunk = n_chunks - 1 - n
        slot = chunk % 2
        for cp in chunk_copies(b, chunk, slot):
            cp.wait()

        @pl.when(chunk > 0)
        def _():
            for cp in chunk_copies(b, chunk - 1, 1 - slot):
                cp.start()

        @pl.when(jnp.logical_and(chunk == 0, b + 1 < nb))
        def _():
            for cp in chunk_copies(b + 1, n_chunks - 1, 1 - slot):
                cp.start()

        z_all = jnp.dot(q_bdb, kbuf[slot].astype(BF16), preferred_element_type=F32) + bias
        sps = [_softplus2(z_all[:, s * sub:(s + 1) * sub]) for s in range(n_sub)]
        sums = [jnp.sum(sp, axis=-1, keepdims=True) for sp in sps]
        cr = car_ref[...]
        a_parts = [None] * n_sub
        for s in range(n_sub - 1, -1, -1):
            later = jnp.dot(sps[s].astype(BF16), upper, preferred_element_type=F32)
            t = z_all[:, s * sub:(s + 1) * sub] - sps[s] - later - jnp.tile(cr, (1, sub // LANES))
            a_parts[s] = jnp.exp2(t).astype(BF16)
            cr = cr + sums[s]
        a_all = jnp.concatenate(
            [jnp.concatenate(a_parts, axis=1), jnp.zeros((LANES - rows, ck), BF16)], axis=0)
        acct_ref[...] += lax.dot_general(vbuf[slot].astype(BF16), a_all, NT_DIMS,
                                         preferred_element_type=F32)
        car_ref[...] = cr
        return carry

    acct_ref[...] = jnp.zeros_like(acct_ref)
    lax.fori_loop(0, n_chunks, chunk_body, 0)

    res = jnp.where(own, acc_ref[...] + acct_ref[...].T[0:rows, :], 0.0)
    o_ref[...] = jnp.concatenate(
        [jnp.sum(res[t * n_heads:(t + 1) * n_heads], axis=0, keepdims=True)
         for t in range(n_tok)], axis=0)


def _attn_sample(page_table, q_nat, k_new, v_new, bias_rows, cache_kt, cache_vt, n_heads):
    n_seq, n_tok, d_sb = q_nat.shape
    n_pages = page_table.shape[1]
    page = cache_kt.shape[2]
    pages_per_chunk = 16
    n_chunks = n_pages // pages_per_chunk
    assert n_chunks * pages_per_chunk == n_pages and n_chunks % 2 == 0
    rows = n_tok * n_heads
    ck = pages_per_chunk * page
    tok_spec = pl.BlockSpec((None, n_tok, d_sb), lambda b, pt: (b, 0, 0))
    grid_spec = pltpu.PrefetchScalarGridSpec(
        num_scalar_prefetch=1,
        grid=(n_seq,),
        in_specs=[tok_spec, tok_spec, tok_spec,
                  pl.BlockSpec((rows, 1), lambda b, pt: (0, 0)),
                  pl.BlockSpec(memory_space=pl.ANY),
                  pl.BlockSpec(memory_space=pl.ANY)],
        out_specs=pl.BlockSpec((None, n_tok, d_sb), lambda b, pt: (b, 0, 0)),
        scratch_shapes=[pltpu.VMEM((2, d_sb, ck), F32),
                        pltpu.VMEM((2, d_sb, ck), F32),
                        pltpu.SemaphoreType.DMA((2, 2)),
                        pltpu.VMEM((rows, d_sb), F32),
                        pltpu.VMEM((d_sb, LANES), F32),
                        pltpu.VMEM((rows, LANES), F32)],
    )
    kern = functools.partial(_attn_sample_kernel, n_tok=n_tok, n_heads=n_heads,
                             pages_per_chunk=pages_per_chunk, n_chunks=n_chunks,
                             page=page, sub=ATTN_BLK)
    return pl.pallas_call(
        kern, grid_spec=grid_spec,
        out_shape=jax.ShapeDtypeStruct((n_seq, n_tok, d_sb), F32),
        compiler_params=_cparams(("arbitrary",)),
        name="attn_sample",
    )(page_table, q_nat, k_new, v_new, bias_rows, cache_kt, cache_vt)


def _outffn_kernel(*refs, tm, d_ff, fc, seq_rows, tiles_per_batch, has_state):
    if has_state:
        (x_ref, osb_ref, oml_ref, g1_ref, sh_ref, sc_ref, g2_ref, n2_ref, wout_ref, wup_ref,
         cw_ref, cb_ref, wdn_ref, p1_ref, p2_ref, y_ref, up_ref) = refs
    else:
        (x_ref, osb_ref, oml_ref, g1_ref, sh_ref, sc_ref, g2_ref, n2_ref, wout_ref, wup_ref,
         cw_ref, cb_ref, wdn_ref, y_ref, st_ref, carry_ref) = refs
    d_sb = osb_ref.shape[1]
    mix = (jnp.dot(osb_ref[...].astype(BF16), wout_ref[0:d_sb, :], preferred_element_type=F32)
           + jnp.dot(oml_ref[...], wout_ref[d_sb:, :], preferred_element_type=F32))
    x1 = x_ref[...] + g1_ref[...] * mix
    ms = jnp.mean(x1 * x1, axis=-1, keepdims=True)
    h2 = x1 * lax.rsqrt(ms + EPS) * n2_ref[...]
    h2 = (h2 * (1.0 + sc_ref[...]) + sh_ref[...]).astype(BF16)

    row = lax.broadcasted_iota(jnp.int32, (tm, 1), 0)
    if has_state:
        t_in = row % seq_rows
    else:
        t_in = row
        first = (pl.program_id(0) % tiles_per_batch) == 0

        @pl.when(first)
        def _():
            carry_ref[...] = jnp.zeros_like(carry_ref)

    def conv(u, cols):
        r1 = pltpu.roll(u, 1, axis=0)
        r2 = pltpu.roll(u, 2, axis=0)
        if has_state:
            up_ref[:, cols] = u
            prev1 = jnp.where(t_in >= 1, r1, p1_ref[:, cols])
            prev2 = jnp.where(t_in >= 2, r2, p2_ref[:, cols])
        else:
            c6 = carry_ref[CARRY_ROWS - 2:CARRY_ROWS - 1, cols]
            c7 = carry_ref[CARRY_ROWS - 1:CARRY_ROWS, cols]
            prev1 = jnp.where(t_in >= 1, r1, c7)
            prev2 = jnp.where(t_in >= 2, r2, jnp.where(t_in == 0, c6, c7))
            carry_ref[:, cols] = u[tm - CARRY_ROWS:, :]
        return (cb_ref[:, cols] + cw_ref[0:1, cols] * prev2 + cw_ref[1:2, cols] * prev1
                + cw_ref[2:3, cols] * u)

    acc = jnp.zeros((tm, x1.shape[1]), F32)
    for f in range(d_ff // fc):
        gc = slice(f * fc, (f + 1) * fc)
        vc = slice(d_ff + f * fc, d_ff + (f + 1) * fc)
        cg = conv(jnp.dot(h2, wup_ref[:, gc], preferred_element_type=F32), gc)
        cv = conv(jnp.dot(h2, wup_ref[:, vc], preferred_element_type=F32), vc)
        act = (cg * jax.nn.sigmoid(cg) * cv).astype(BF16)
        acc = acc + jnp.dot(act, wdn_ref[gc, :], preferred_element_type=F32)
    y_ref[...] = x1 + g2_ref[...] * acc

    if not has_state:
        @pl.when((pl.program_id(0) % tiles_per_batch) == tiles_per_batch - 1)
        def _():
            st_ref[...] = carry_ref[...]


def _outffn(x2, osb, oml, mod, per_row, n_batch, seq_rows, tm, n2, wout_b, wup_b, conv_w, conv_b,
            wdn_b, state_rows):
    rows, d = x2.shape
    d_ff = wdn_b.shape[0]
    d_sb = osb.shape[1]
    d_mlp = oml.shape[1]
    tiles = rows // tm
    tpb = max(tiles // n_batch, 1)
    has_state = state_rows is not None
    const = lambda i: (0, 0)
    row_spec = lambda w: pl.BlockSpec((tm, w), lambda i: (i, 0))
    in_specs = [
        row_spec(d), row_spec(d_sb), row_spec(d_mlp),
        _mod_spec(per_row, 2, tm, d, tpb),
        _mod_spec(per_row, 3, tm, d, tpb),
        _mod_spec(per_row, 4, tm, d, tpb),
        _mod_spec(per_row, 5, tm, d, tpb),
        pl.BlockSpec((1, d), const),
        _resident(wout_b.shape),
        _resident(wup_b.shape),
        pl.BlockSpec(conv_w.shape, const),
        pl.BlockSpec((1, 2 * d_ff), const),
        _resident(wdn_b.shape),
    ]
    args = [x2, osb, oml, mod, mod, mod, mod, n2, wout_b, wup_b, conv_w, conv_b, wdn_b]
    if has_state:
        in_specs += [_resident((rows, 2 * d_ff)), _resident((rows, 2 * d_ff))]
        args += list(state_rows)
        out_specs = [row_spec(d), row_spec(2 * d_ff)]
        out_shape = [jax.ShapeDtypeStruct((rows, d), F32),
                     jax.ShapeDtypeStruct((rows, 2 * d_ff), F32)]
        scratch = []
    else:
        out_specs = [row_spec(d),
                     pl.BlockSpec((None, CARRY_ROWS, 2 * d_ff), lambda i: (i // tpb, 0, 0))]
        out_shape = [jax.ShapeDtypeStruct((rows, d), F32),
                     jax.ShapeDtypeStruct((n_batch, CARRY_ROWS, 2 * d_ff), F32)]
        scratch = [pltpu.VMEM((CARRY_ROWS, 2 * d_ff), F32)]
    kern = functools.partial(_outffn_kernel, tm=tm, d_ff=d_ff, fc=d_ff // 2, seq_rows=seq_rows,
                             tiles_per_batch=tpb, has_state=has_state)
    return pl.pallas_call(
        kern, grid=(tiles,), in_specs=in_specs, out_specs=out_specs, out_shape=out_shape,
        scratch_shapes=scratch,
        compiler_params=_cparams(("arbitrary",)),
        name="outffn_sample" if has_state else "outffn_prompt",
    )(*args)


def _layer(l, yp, ys, c_prompt, c_sample, cache_k, cache_v, state_conv, page_table,
           norm1_g, norm2_g, w_ada, b_ada, w_in, q_norm_g, k_norm_g, sb_bias, v_norm_g,
           w_s, b_s, w_out, w_up, conv_w, conv_b, w_down):
    bp, tp, d = yp.shape
    db, ts, _ = ys.shape
    d_ff = w_down.shape[1]
    n_heads = sb_bias.shape[1]
    d_sb = n_heads * HEAD_DIM
    n_grp = v_norm_g.shape[1]
    d_mlp = n_grp * CHUNK
    n_pool, page = cache_k.shape[1], cache_k.shape[2]

    n_c = bp + db
    pad = (-n_c) % 8
    c_all = jnp.concatenate([c_prompt, c_sample, jnp.zeros((pad, d), F32)], axis=0)
    mod = _ada(c_all, w_ada[l], b_ada[l])
    mod_p = mod[:bp].reshape(bp, N_MOD, 1, d)
    mod_s = jnp.repeat(mod[bp:n_c].reshape(db, N_MOD, d).transpose(1, 0, 2), ts, axis=1)

    win_b = w_in[l].astype(BF16)
    wnat_b = jnp.concatenate([win_b[:, :d_sb], win_b[:, 2 * d_sb:]], axis=1)
    wkvt_b = win_b[:, d_sb:3 * d_sb].T
    wout_b = w_out[l].astype(BF16)
    wup_b = w_up[l].astype(BF16)
    wdn_b = w_down[l].astype(BF16)
    g1 = norm1_g[l].reshape(1, d)
    n2 = norm2_g[l].reshape(1, d)
    gq = (jnp.tile(q_norm_g[l], n_heads) * (LOG2E * HEAD_DIM ** -0.5)).reshape(1, d_sb)
    sbb2 = sb_bias[l] * LOG2E
    gk = jnp.tile(k_norm_g[l], n_heads).reshape(1, d_sb)
    gkc = jnp.broadcast_to(k_norm_g[l][:, None], (HEAD_DIM, LANES))
    gv = v_norm_g[l].reshape(1, d_mlp)
    qaux = jnp.where((jnp.arange(LANES) == HEAD_DIM)[None, :], sbb2[:, None], 0.0).astype(F32)
    cb = conv_b[l].reshape(1, 2 * d_ff)

    bmix_p = jnp.repeat(b_s[l].T, CHUNK, axis=1)
    xp2 = yp.reshape(bp * tp, d)
    kt, vt, qh, kh, vb, oml_p = _inproj_prompt(
        xp2, mod_p, bp, ATTN_BLK, wnat_b, wkvt_b, g1, gq, gkc, gv, qaux, w_s[l], bmix_p,
        d_sb, d_mlp)
    osb_p = _attn_prompt(qh, kh, vb.reshape(bp, tp, d_sb)).reshape(bp * tp, d_sb)
    yp2, st_p = _outffn(xp2, osb_p, oml_p, mod_p, False, bp, tp, 256, n2, wout_b, wup_b,
                        conv_w[l], cb, wdn_b, None)

    rows_s = db * ts
    wmix_s = jnp.tile(w_s[l][:, :ts, :ts], (1, CHUNK // ts, CHUNK // ts))
    bmix_s = jnp.repeat(jnp.tile(b_s[l][:, :ts], (1, CHUNK // ts)).T, CHUNK, axis=1)
    xs2 = ys.reshape(rows_s, d)
    ks_, vs_, qn_s, gv_s, oml_s = _inproj_sample(
        xs2, mod_s, ts, win_b, g1, gq, gk, gv, wmix_s, bmix_s, d_sb, d_mlp)
    bias_rows = jnp.tile(sbb2, ts).reshape(ts * n_heads, 1)
    cache_kt = cache_k[l].transpose(0, 2, 3, 1).reshape(n_pool, d_sb, page)
    cache_vt = cache_v[l].transpose(0, 2, 3, 1).reshape(n_pool, d_sb, page)
    osb_s = _attn_sample(page_table, qn_s.reshape(db, ts, d_sb), ks_.reshape(db, ts, d_sb),
                         vs_.reshape(db, ts, d_sb), bias_rows, cache_kt, cache_vt, n_heads)
    st = state_conv[l]
    p1 = jnp.concatenate([st[:, 1:2], jnp.zeros((db, ts - 1, 2 * d_ff), F32)], axis=1)
    p2 = jnp.concatenate([st, jnp.zeros((db, ts - 2, 2 * d_ff), F32)], axis=1)
    ys2, up_s = _outffn(xs2, osb_s.reshape(rows_s, d_sb), oml_s, mod_s, True, db, ts, rows_s, n2,
                        wout_b, wup_b, conv_w[l], cb, wdn_b,
                        (p1.reshape(rows_s, 2 * d_ff), p2.reshape(rows_s, 2 * d_ff)))

    outs = (kt.reshape(bp, n_heads, HEAD_DIM, tp).transpose(0, 3, 1, 2),
            vt.reshape(bp, n_heads, HEAD_DIM, tp).transpose(0, 3, 1, 2),
            ks_.reshape(db, ts, n_heads, HEAD_DIM), vs_.reshape(db, ts, n_heads, HEAD_DIM),
            gv_s.reshape(db, ts, n_grp, CHUNK),
            st_p[:, CARRY_ROWS - (CONV_W - 1):],
            up_s.reshape(db, ts, 2 * d_ff)[:, ts - (CONV_W - 1):])
    return yp2.reshape(bp, tp, d), ys2.reshape(db, ts, d), outs


def kernel(x_prompt, x_sample, c_prompt, c_sample, cache_k, cache_v, state_conv, page_table,
           norm1_g, norm2_g, w_ada, b_ada, w_in, q_norm_g, k_norm_g, sb_bias, v_norm_g,
           w_s, b_s, w_out, w_up, conv_w, conv_b, w_down):
    depth = w_in.shape[0]
    yp, ys = x_prompt, x_sample
    per_layer = []
    for l in range(depth):
        yp, ys, outs = _layer(l, yp, ys, c_prompt, c_sample, cache_k, cache_v, state_conv,
                              page_table, norm1_g, norm2_g, w_ada, b_ada, w_in, q_norm_g,
                              k_norm_g, sb_bias, v_norm_g, w_s, b_s, w_out, w_up, conv_w,
                              conv_b, w_down)
        per_layer.append(outs)
    stacked = tuple(jnp.stack([o[i] for o in per_layer]) for i in range(7))
    return (yp, ys) + stacked
```

```python
import functools

import jax
import jax.numpy as jnp
from jax import lax
from jax.experimental import pallas as pl
from jax.experimental.pallas import tpu as pltpu

F32 = jnp.float32
BF16 = jnp.bfloat16

EPS = 1e-6
LOG2E = 1.4426950408889634
HEAD_DIM = 64
LANES = 128
CHUNK = 128
N_MOD = 6
CONV_W = 3
CARRY_ROWS = 8
ATTN_BLK = 256
HEADS_PER_STEP = 8
SAMPLE_CHUNK_PAGES = 16
SAMPLE_RING_SLOTS = 3

VMEM_LIMIT = 56 * 1024 * 1024

NT_DIMS = (((1,), (1,)), ((), ()))


def _cparams(sem):
    return pltpu.CompilerParams(dimension_semantics=sem, vmem_limit_bytes=VMEM_LIMIT)


def _resident(shape):
    nd = len(shape)
    return pl.BlockSpec(shape, lambda *_: (0,) * nd, pipeline_mode=pl.Buffered(1))


def _mod_spec(per_row, k, tm, d, tiles_per_batch):
    if per_row:
        return pl.BlockSpec((None, tm, d), lambda i: (k, i, 0))
    return pl.BlockSpec((None, None, 1, d), lambda i: (i // tiles_per_batch, k, 0, 0))


def _ada_kernel(c_ref, w_ref, b_ref, o_ref):
    c = c_ref[...]
    s = c * jax.nn.sigmoid(c)
    o_ref[...] = jnp.dot(s, w_ref[...], preferred_element_type=F32,
                         precision=lax.Precision.HIGHEST) + b_ref[...]


def _ada(c_all, w_ada, b_ada):
    rows, d = c_all.shape
    n = w_ada.shape[1]
    bn = n // 4
    return pl.pallas_call(
        _ada_kernel,
        grid=(n // bn,),
        in_specs=[pl.BlockSpec((rows, d), lambda j: (0, 0)),
                  pl.BlockSpec((d, bn), lambda j: (0, j)),
                  pl.BlockSpec((1, bn), lambda j: (0, j))],
        out_specs=pl.BlockSpec((rows, bn), lambda j: (0, j)),
        out_shape=jax.ShapeDtypeStruct((rows, n), F32),
        compiler_params=_cparams(("arbitrary",)),
        name="ada",
    )(c_all, w_ada, b_ada.reshape(1, n))


def _two_head_norm(blk, g):
    lo = lax.broadcasted_iota(jnp.int32, blk.shape, 1) < HEAD_DIM
    sq = blk * blk
    s_lo = jnp.sum(jnp.where(lo, sq, 0.0), axis=-1, keepdims=True)
    s_hi = jnp.sum(jnp.where(lo, 0.0, sq), axis=-1, keepdims=True)
    r = lax.rsqrt(jnp.where(lo, s_lo, s_hi) * (1.0 / HEAD_DIM) + EPS)
    return blk * r * g


def _spatial_gate(proj, u0, g0, gv_ref, wmix_ref, bmix_ref, om_ref, gvo_ref, tm, n_grp, seq_rows):
    r_i = lax.broadcasted_iota(jnp.int32, (CHUNK, CHUNK), 0)
    c_i = lax.broadcasted_iota(jnp.int32, (CHUNK, CHUNK), 1)
    if seq_rows >= CHUNK:
        keep = r_i >= c_i
    else:
        keep = jnp.logical_and(r_i >= c_i, c_i >= (r_i // seq_rows) * seq_rows)
    for g in range(n_grp):
        cs = slice(g * CHUNK, (g + 1) * CHUNK)
        wm = jnp.where(keep, wmix_ref[g], 0.0).astype(BF16)
        u = jax.nn.gelu(proj[:, u0 + g * CHUNK:u0 + (g + 1) * CHUNK])
        vg = jax.nn.gelu(proj[:, g0 + g * CHUNK:g0 + (g + 1) * CHUNK])
        vn = vg * lax.rsqrt(jnp.mean(vg * vg, axis=-1, keepdims=True) + EPS) * gv_ref[:, cs]
        if gvo_ref is not None:
            gvo_ref[:, cs] = vn
        vnb = vn.astype(BF16)
        for c in range(tm // CHUNK):
            rs = slice(c * CHUNK, (c + 1) * CHUNK)
            mixed = jnp.dot(wm, vnb[rs], preferred_element_type=F32) + bmix_ref[:, cs]
            om_ref[rs, cs] = (u[rs] * mixed).astype(BF16)


def _normed_input(x_ref, sh_ref, sc_ref, g_ref):
    x = x_ref[...]
    ms = jnp.mean(x * x, axis=-1, keepdims=True)
    h = x * lax.rsqrt(ms + EPS) * g_ref[...]
    return (h * (1.0 + sc_ref[...]) + sh_ref[...]).astype(BF16)


def _inproj_prompt_kernel(x_ref, sh_ref, sc_ref, g1_ref, wnat_ref, wkvt_ref, gq_ref, gkc_ref,
                          gv_ref, qaux_ref, wmix_ref, bmix_ref,
                          kt_ref, vt_ref, qh_ref, kh_ref, vb_ref, om_ref,
                          *, tm, d_sb, d_mlp, seq_rows):
    n_heads = d_sb // HEAD_DIM
    hb = _normed_input(x_ref, sh_ref, sc_ref, g1_ref)
    proj = jnp.dot(hb, wnat_ref[...], preferred_element_type=F32)
    kvt = lax.dot_general(wkvt_ref[...], hb, NT_DIMS, preferred_element_type=F32)

    gkc = jnp.tile(gkc_ref[...], (1, tm // LANES))
    aux_rows = jnp.where(lax.broadcasted_iota(jnp.int32, (HEAD_DIM, tm), 0) == 0, 1.0, 0.0)
    for hd in range(n_heads):
        rs = slice(hd * HEAD_DIM, (hd + 1) * HEAD_DIM)
        blk = kvt[rs, :]
        kn = blk * lax.rsqrt(jnp.mean(blk * blk, axis=0, keepdims=True) + EPS) * gkc
        kt_ref[rs, :] = kn
        kh_ref[hd, 0:HEAD_DIM, :] = kn.astype(BF16)
        kh_ref[hd, HEAD_DIM:, :] = aux_rows.astype(BF16)
    vt_ref[...] = kvt[d_sb:, :]

    lo = lax.broadcasted_iota(jnp.int32, (tm, LANES), 1) < HEAD_DIM
    for p in range(d_sb // LANES):
        cs = slice(p * LANES, (p + 1) * LANES)
        qn = _two_head_norm(proj[:, p * LANES:(p + 1) * LANES], gq_ref[:, cs])
        vb_ref[:, cs] = proj[:, d_sb + p * LANES:d_sb + (p + 1) * LANES].astype(BF16)
        qh_ref[2 * p] = jnp.where(lo, qn, qaux_ref[2 * p:2 * p + 1, :]).astype(BF16)
        qr = pltpu.roll(qn, HEAD_DIM, axis=1)
        qh_ref[2 * p + 1] = jnp.where(lo, qr, qaux_ref[2 * p + 1:2 * p + 2, :]).astype(BF16)

    _spatial_gate(proj, 2 * d_sb, 2 * d_sb + d_mlp, gv_ref, wmix_ref, bmix_ref, om_ref, None,
                  tm, d_mlp // CHUNK, seq_rows)


def _inproj_sample_kernel(x_ref, sh_ref, sc_ref, g1_ref, win_ref, gq_ref, gk_ref, gv_ref,
                          wmix_ref, bmix_ref, k_ref, v_ref, qn_ref, gvo_ref, om_ref,
                          *, tm, d_sb, d_mlp, seq_rows):
    hb = _normed_input(x_ref, sh_ref, sc_ref, g1_ref)
    proj = jnp.dot(hb, win_ref[...], preferred_element_type=F32)
    for p in range(d_sb // LANES):
        cs = slice(p * LANES, (p + 1) * LANES)
        qn_ref[:, cs] = _two_head_norm(proj[:, p * LANES:(p + 1) * LANES], gq_ref[:, cs])
        k_ref[:, cs] = _two_head_norm(proj[:, d_sb + p * LANES:d_sb + (p + 1) * LANES],
                                      gk_ref[:, cs])
        v_ref[:, cs] = proj[:, 2 * d_sb + p * LANES:2 * d_sb + (p + 1) * LANES]
    _spatial_gate(proj, 3 * d_sb, 3 * d_sb + d_mlp, gv_ref, wmix_ref, bmix_ref, om_ref, gvo_ref,
                  tm, d_mlp // CHUNK, seq_rows)


def _inproj_prompt(x2, mod, n_batch, tm, wnat_b, wkvt_b, g1, gq, gkc, gv, qaux, wmix, bmix,
                   d_sb, d_mlp):
    rows, d = x2.shape
    n_heads = d_sb // HEAD_DIM
    t_len = rows // n_batch
    tpb = t_len // tm
    const = lambda i: (0, 0)
    in_specs = [
        pl.BlockSpec((tm, d), lambda i: (i, 0)),
        _mod_spec(False, 0, tm, d, tpb),
        _mod_spec(False, 1, tm, d, tpb),
        pl.BlockSpec((1, d), const),
        _resident(wnat_b.shape),
        _resident(wkvt_b.shape),
        pl.BlockSpec((1, d_sb), const),
        pl.BlockSpec((HEAD_DIM, LANES), const),
        pl.BlockSpec((1, d_mlp), const),
        pl.BlockSpec((n_heads, LANES), const),
        pl.BlockSpec(wmix.shape, lambda i: (0, 0, 0)),
        pl.BlockSpec((CHUNK, d_mlp), const),
    ]
    row_spec = lambda w: pl.BlockSpec((tm, w), lambda i: (i, 0))
    tmin_spec = pl.BlockSpec((None, d_sb, tm), lambda i: (i // tpb, 0, i % tpb))
    out_specs = [
        tmin_spec, tmin_spec,
        pl.BlockSpec((None, n_heads, tm, LANES), lambda i: (i // tpb, 0, i % tpb, 0)),
        pl.BlockSpec((None, n_heads, None, LANES, tm), lambda i: (i // tpb, 0, i % tpb, 0, 0)),
        row_spec(d_sb), row_spec(d_mlp)]
    out_shape = [
        jax.ShapeDtypeStruct((n_batch, d_sb, t_len), F32),
        jax.ShapeDtypeStruct((n_batch, d_sb, t_len), F32),
        jax.ShapeDtypeStruct((n_batch, n_heads, t_len, LANES), BF16),
        jax.ShapeDtypeStruct((n_batch, n_heads, tpb, LANES, tm), BF16),
        jax.ShapeDtypeStruct((rows, d_sb), BF16),
        jax.ShapeDtypeStruct((rows, d_mlp), BF16)]
    kern = functools.partial(_inproj_prompt_kernel, tm=tm, d_sb=d_sb, d_mlp=d_mlp, seq_rows=t_len)
    return pl.pallas_call(
        kern, grid=(rows // tm,), in_specs=in_specs, out_specs=out_specs, out_shape=out_shape,
        compiler_params=_cparams(("arbitrary",)), name="inproj_prompt",
    )(x2, mod, mod, g1, wnat_b, wkvt_b, gq, gkc, gv, qaux, wmix, bmix)


def _inproj_sample(x2, mod, seq_rows, win_b, g1, gq, gk, gv, wmix, bmix, d_sb, d_mlp):
    rows, d = x2.shape
    tm = rows
    const = lambda i: (0, 0)
    in_specs = [
        pl.BlockSpec((tm, d), lambda i: (i, 0)),
        _mod_spec(True, 0, tm, d, 1),
        _mod_spec(True, 1, tm, d, 1),
        pl.BlockSpec((1, d), const),
        _resident(win_b.shape),
        pl.BlockSpec((1, d_sb), const),
        pl.BlockSpec((1, d_sb), const),
        pl.BlockSpec((1, d_mlp), const),
        pl.BlockSpec(wmix.shape, lambda i: (0, 0, 0)),
        pl.BlockSpec((CHUNK, d_mlp), const),
    ]
    row_spec = lambda w: pl.BlockSpec((tm, w), lambda i: (i, 0))
    out_specs = [row_spec(d_sb), row_spec(d_sb), row_spec(d_sb), row_spec(d_mlp), row_spec(d_mlp)]
    out_shape = [jax.ShapeDtypeStruct((rows, d_sb), F32),
                 jax.ShapeDtypeStruct((rows, d_sb), F32),
                 jax.ShapeDtypeStruct((rows, d_sb), F32),
                 jax.ShapeDtypeStruct((rows, d_mlp), F32),
                 jax.ShapeDtypeStruct((rows, d_mlp), BF16)]
    kern = functools.partial(_inproj_sample_kernel, tm=tm, d_sb=d_sb, d_mlp=d_mlp,
                             seq_rows=seq_rows)
    return pl.pallas_call(
        kern, grid=(1,), in_specs=in_specs, out_specs=out_specs, out_shape=out_shape,
        compiler_params=_cparams(("arbitrary",)), name="inproj_sample",
    )(x2, mod, mod, g1, win_b, gq, gk, gv, wmix, bmix)


def _softplus2(z2):
    return jnp.maximum(z2, 0.0) + jnp.log(1.0 + jnp.exp2(-jnp.abs(z2))) * LOG2E


def _strict_upper(n):
    r = lax.broadcasted_iota(jnp.int32, (n, n), 0)
    c = lax.broadcasted_iota(jnp.int32, (n, n), 1)
    return jnp.where(r > c, 1.0, 0.0).astype(BF16)


def _attn_prompt_kernel(q_ref, k_ref, v_ref, o_ref, acc_ref, car_ref, *, blk, n_h):
    i = pl.program_id(2)
    upper = _strict_upper(blk)
    reps = blk // LANES
    row = lax.broadcasted_iota(jnp.int32, (blk, blk), 0)
    col = lax.broadcasted_iota(jnp.int32, (blk, blk), 1)
    causal = col < row

    def tiles(j, mask):
        heads = range(n_h)
        ks = pl.multiple_of(j * blk, blk)
        zs = [jnp.dot(q_ref[hh], k_ref[hh, j], preferred_element_type=F32) for hh in heads]
        sps = [_softplus2(z) for z in zs]
        if mask is not None:
            sps = [jnp.where(mask, sp, 0.0) for sp in sps]
        lbs = [zs[hh] - sps[hh] for hh in heads]
        laters = [jnp.dot(sp.astype(BF16), upper, preferred_element_type=F32) for sp in sps]
        cars = [car_ref[hh] for hh in heads]
        for hh in heads:
            car_ref[hh] = cars[hh] + jnp.sum(sps[hh], axis=-1, keepdims=True)
        a_s = [jnp.exp2(lbs[hh] - laters[hh] - jnp.tile(cars[hh], (1, reps))) for hh in heads]
        if mask is not None:
            a_s = [jnp.where(mask, a, 0.0) for a in a_s]
        for hh in heads:
            v = v_ref[pl.ds(ks, blk), (hh // 2) * LANES:(hh // 2 + 1) * LANES]
            acc_ref[hh] += jnp.dot(a_s[hh].astype(BF16), v, preferred_element_type=F32)

    acc_ref[...] = jnp.zeros_like(acc_ref)
    car_ref[...] = jnp.zeros_like(car_ref)
    tiles(i, causal)

    def body(n, c):
        tiles(i - 1 - n, None)
        return c

    lax.fori_loop(0, i, body, 0)

    lo = lax.broadcasted_iota(jnp.int32, (blk, LANES), 1) < HEAD_DIM
    for p in range(n_h // 2):
        o_ref[:, p * LANES:(p + 1) * LANES] = jnp.where(
            lo, acc_ref[2 * p], acc_ref[2 * p + 1]).astype(o_ref.dtype)


def _attn_prompt(qh, kh, vb):
    n_batch, n_heads, t_len, _ = qh.shape
    blk = kh.shape[-1]
    n_h = HEADS_PER_STEP
    d_sb = n_heads * HEAD_DIM
    w = n_h * HEAD_DIM
    return pl.pallas_call(
        functools.partial(_attn_prompt_kernel, blk=blk, n_h=n_h),
        grid=(n_batch, n_heads // n_h, t_len // blk),
        in_specs=[pl.BlockSpec((None, n_h, blk, LANES), lambda b, g, i: (b, g, i, 0)),
                  pl.BlockSpec((None, n_h, t_len // blk, LANES, blk),
                               lambda b, g, i: (b, g, 0, 0, 0)),
                  pl.BlockSpec((None, t_len, w), lambda b, g, i: (b, 0, g))],
        out_specs=pl.BlockSpec((None, blk, w), lambda b, g, i: (b, i, g)),
        out_shape=jax.ShapeDtypeStruct((n_batch, t_len, d_sb), BF16),
        scratch_shapes=[pltpu.VMEM((n_h, blk, LANES), F32), pltpu.VMEM((n_h, blk, LANES), F32)],
        compiler_params=_cparams(("arbitrary", "arbitrary", "arbitrary")),
        name="attn_prompt",
    )(qh, kh, vb)


def _attn_sample_kernel(pt_ref, q_ref, kn_ref, vn_ref, bias_ref, ck_hbm, cv_hbm, o_ref,
                        kbuf, vbuf, sem, acc_ref, acct_ref, car_ref,
                        *, n_tok, n_heads, pages_per_chunk, n_chunks, n_buf, page, sub):
    b = pl.program_id(0)
    nb = pl.num_programs(0)
    d_sb = n_heads * HEAD_DIM
    rows = n_tok * n_heads
    ck = pages_per_chunk * page
    n_sub = ck // sub

    def chunk_copies(g):
        seq = g // n_chunks
        chunk = n_chunks - 1 - g % n_chunks
        slot = g % n_buf
        cps = []
        for pg in range(pages_per_chunk):
            pid = pt_ref[seq, chunk * pages_per_chunk + pg]
            dst = pl.ds(pg * page, page)
            cps.append(pltpu.make_async_copy(ck_hbm.at[pid], kbuf.at[slot, :, dst], sem.at[0, slot]))
            cps.append(pltpu.make_async_copy(cv_hbm.at[pid], vbuf.at[slot, :, dst], sem.at[1, slot]))
        return cps

    @pl.when(b == 0)
    def _():
        for g0 in range(n_buf - 1):
            for cp in chunk_copies(g0):
                cp.start()

    r_i = lax.broadcasted_iota(jnp.int32, (rows, d_sb), 0)
    c_i = lax.broadcasted_iota(jnp.int32, (rows, d_sb), 1)
    own = (r_i % n_heads) == (c_i // HEAD_DIM)
    q_nat = q_ref[...]
    q_rep = jnp.concatenate(
        [jnp.broadcast_to(q_nat[t:t + 1, :], (n_heads, d_sb)) for t in range(n_tok)], axis=0)
    q_bd = jnp.where(own, q_rep, 0.0)
    q_bdb = q_bd.astype(BF16)
    bias = bias_ref[...]
    tok = lax.broadcasted_iota(jnp.int32, (rows, 1), 0) // n_heads

    acc = jnp.zeros((rows, d_sb), F32)
    car = jnp.zeros((rows, 1), F32)
    k_new = kn_ref[...]
    v_new = vn_ref[...]
    for s in range(n_tok - 1, -1, -1):
        z = jnp.sum(q_bd * k_new[s:s + 1, :], axis=-1, keepdims=True) + bias
        vis = tok > s
        sp = jnp.where(vis, _softplus2(z), 0.0)
        a = jnp.where(vis, jnp.exp2(z - sp - car), 0.0)
        acc = acc + a * v_new[s:s + 1, :]
        car = car + sp
    acc_ref[...] = acc
    car_ref[...] = jnp.broadcast_to(car, (rows, LANES))

    upper = _strict_upper(sub)

    def chunk_body(n, carry):
        g = b * n_chunks + n
        slot = g % n_buf
        for cp in chunk_copies(g):
            cp.wait()

        @pl.when(g + n_buf - 1 < nb * n_chunks)
        def _():
            for cp in chunk_copies(g + n_buf - 1):
                cp.start()

        z_all = jnp.dot(q_bdb, kbuf[slot].astype(BF16), preferred_element_type=F32) + bias
        sps = [_softplus2(z_all[:, s * sub:(s + 1) * sub]) for s in range(n_sub)]
        sums = [jnp.sum(sp, axis=-1, keepdims=True) for sp in sps]
        cr = car_ref[...]
        a_parts = [None] * n_sub
        for s in range(n_sub - 1, -1, -1):
            later = jnp.dot(sps[s].astype(BF16), upper, preferred_element_type=F32)
            t = z_all[:, s * sub:(s + 1) * sub] - sps[s] - later - jnp.tile(cr, (1, sub // LANES))
            a_parts[s] = jnp.exp2(t).astype(BF16)
            cr = cr + sums[s]
        a_all = jnp.concatenate(
            [jnp.concatenate(a_parts, axis=1), jnp.zeros((LANES - rows, ck), BF16)], axis=0)
        acct_ref[...] += lax.dot_general(vbuf[slot].astype(BF16), a_all, NT_DIMS,
                                         preferred_element_type=F32)
        car_ref[...] = cr
        return carry

    acct_ref[...] = jnp.zeros_like(acct_ref)
    lax.fori_loop(0, n_chunks, chunk_body, 0)

    res = jnp.where(own, acc_ref[...] + acct_ref[...].T[0:rows, :], 0.0)
    o_ref[...] = jnp.concatenate(
        [jnp.sum(res[t * n_heads:(t + 1) * n_heads], axis=0, keepdims=True)
         for t in range(n_tok)], axis=0)


def _attn_sample(page_table, q_nat, k_new, v_new, bias_rows, cache_kt, cache_vt, n_heads):
    n_seq, n_tok, d_sb = q_nat.shape
    n_pages = page_table.shape[1]
    page = cache_kt.shape[2]
    pages_per_chunk = SAMPLE_CHUNK_PAGES
    n_buf = SAMPLE_RING_SLOTS
    n_chunks = n_pages // pages_per_chunk
    assert n_chunks * pages_per_chunk == n_pages
    rows = n_tok * n_heads
    ck = pages_per_chunk * page
    tok_spec = pl.BlockSpec((None, n_tok, d_sb), lambda b, pt: (b, 0, 0))
    grid_spec = pltpu.PrefetchScalarGridSpec(
        num_scalar_prefetch=1,
        grid=(n_seq,),
        in_specs=[tok_spec, tok_spec, tok_spec,
                  pl.BlockSpec((rows, 1), lambda b, pt: (0, 0)),
                  pl.BlockSpec(memory_space=pl.ANY),
                  pl.BlockSpec(memory_space=pl.ANY)],
        out_specs=pl.BlockSpec((None, n_tok, d_sb), lambda b, pt: (b, 0, 0)),
        scratch_shapes=[pltpu.VMEM((n_buf, d_sb, ck), F32),
                        pltpu.VMEM((n_buf, d_sb, ck), F32),
                        pltpu.SemaphoreType.DMA((2, n_buf)),
                        pltpu.VMEM((rows, d_sb), F32),
                        pltpu.VMEM((d_sb, LANES), F32),
                        pltpu.VMEM((rows, LANES), F32)],
    )
    kern = functools.partial(_attn_sample_kernel, n_tok=n_tok, n_heads=n_heads,
                             pages_per_chunk=pages_per_chunk, n_chunks=n_chunks, n_buf=n_buf,
                             page=page, sub=ATTN_BLK)
    return pl.pallas_call(
        kern, grid_spec=grid_spec,
        out_shape=jax.ShapeDtypeStruct((n_seq, n_tok, d_sb), F32),
        compiler_params=_cparams(("arbitrary",)),
        name="attn_sample",
    )(page_table, q_nat, k_new, v_new, bias_rows, cache_kt, cache_vt)


def _outffn_kernel(*refs, tm, d_ff, fc, seq_rows, tiles_per_batch, has_state):
    if has_state:
        (x_ref, osb_ref, oml_ref, g1_ref, sh_ref, sc_ref, g2_ref, n2_ref, wout_ref, wup_ref,
         cw_ref, cb_ref, wdn_ref, p1_ref, p2_ref, y_ref, up_ref) = refs
    else:
        (x_ref, osb_ref, oml_ref, g1_ref, sh_ref, sc_ref, g2_ref, n2_ref, wout_ref, wup_ref,
         cw_ref, cb_ref, wdn_ref, y_ref, st_ref, carry_ref) = refs
    d_sb = osb_ref.shape[1]
    mix = (jnp.dot(osb_ref[...].astype(BF16), wout_ref[0:d_sb, :], preferred_element_type=F32)
           + jnp.dot(oml_ref[...], wout_ref[d_sb:, :], preferred_element_type=F32))
    x1 = x_ref[...] + g1_ref[...] * mix
    ms = jnp.mean(x1 * x1, axis=-1, keepdims=True)
    h2 = x1 * lax.rsqrt(ms + EPS) * n2_ref[...]
    h2 = (h2 * (1.0 + sc_ref[...]) + sh_ref[...]).astype(BF16)

    row = lax.broadcasted_iota(jnp.int32, (tm, 1), 0)
    if has_state:
        t_in = row % seq_rows
    else:
        t_in = row
        first = (pl.program_id(0) % tiles_per_batch) == 0

        @pl.when(first)
        def _():
            carry_ref[...] = jnp.zeros_like(carry_ref)

    def conv(u, cols):
        r1 = pltpu.roll(u, 1, axis=0)
        r2 = pltpu.roll(u, 2, axis=0)
        if has_state:
            up_ref[:, cols] = u
            prev1 = jnp.where(t_in >= 1, r1, p1_ref[:, cols])
            prev2 = jnp.where(t_in >= 2, r2, p2_ref[:, cols])
        else:
            c6 = carry_ref[CARRY_ROWS - 2:CARRY_ROWS - 1, cols]
            c7 = carry_ref[CARRY_ROWS - 1:CARRY_ROWS, cols]
            prev1 = jnp.where(t_in >= 1, r1, c7)
            prev2 = jnp.where(t_in >= 2, r2, jnp.where(t_in == 0, c6, c7))
            carry_ref[:, cols] = u[tm - CARRY_ROWS:, :]
        return (cb_ref[:, cols] + cw_ref[0:1, cols] * prev2 + cw_ref[1:2, cols] * prev1
                + cw_ref[2:3, cols] * u)

    acc = jnp.zeros((tm, x1.shape[1]), F32)
    for f in range(d_ff // fc):
        gc = slice(f * fc, (f + 1) * fc)
        vc = slice(d_ff + f * fc, d_ff + (f + 1) * fc)
        cg = conv(jnp.dot(h2, wup_ref[:, gc], preferred_element_type=F32), gc)
        cv = conv(jnp.dot(h2, wup_ref[:, vc], preferred_element_type=F32), vc)
        act = (cg * jax.nn.sigmoid(cg) * cv).astype(BF16)
        acc = acc + jnp.dot(act, wdn_ref[gc, :], preferred_element_type=F32)
    y_ref[...] = x1 + g2_ref[...] * acc

    if not has_state:
        @pl.when((pl.program_id(0) % tiles_per_batch) == tiles_per_batch - 1)
        def _():
            st_ref[...] = carry_ref[...]


def _outffn(x2, osb, oml, mod, per_row, n_batch, seq_rows, tm, n2, wout_b, wup_b, conv_w, conv_b,
            wdn_b, state_rows):
    rows, d = x2.shape
    d_ff = wdn_b.shape[0]
    d_sb = osb.shape[1]
    d_mlp = oml.shape[1]
    tiles = rows // tm
    tpb = max(tiles // n_batch, 1)
    has_state = state_rows is not None
    const = lambda i: (0, 0)
    row_spec = lambda w: pl.BlockSpec((tm, w), lambda i: (i, 0))
    in_specs = [
        row_spec(d), row_spec(d_sb), row_spec(d_mlp),
        _mod_spec(per_row, 2, tm, d, tpb),
        _mod_spec(per_row, 3, tm, d, tpb),
        _mod_spec(per_row, 4, tm, d, tpb),
        _mod_spec(per_row, 5, tm, d, tpb),
        pl.BlockSpec((1, d), const),
        _resident(wout_b.shape),
        _resident(wup_b.shape),
        pl.BlockSpec(conv_w.shape, const),
        pl.BlockSpec((1, 2 * d_ff), const),
        _resident(wdn_b.shape),
    ]
    args = [x2, osb, oml, mod, mod, mod, mod, n2, wout_b, wup_b, conv_w, conv_b, wdn_b]
    if has_state:
        in_specs += [_resident((rows, 2 * d_ff)), _resident((rows, 2 * d_ff))]
        args += list(state_rows)
        out_specs = [row_spec(d), row_spec(2 * d_ff)]
        out_shape = [jax.ShapeDtypeStruct((rows, d), F32),
                     jax.ShapeDtypeStruct((rows, 2 * d_ff), F32)]
        scratch = []
    else:
        out_specs = [row_spec(d),
                     pl.BlockSpec((None, CARRY_ROWS, 2 * d_ff), lambda i: (i // tpb, 0, 0))]
        out_shape = [jax.ShapeDtypeStruct((rows, d), F32),
                     jax.ShapeDtypeStruct((n_batch, CARRY_ROWS, 2 * d_ff), F32)]
        scratch = [pltpu.VMEM((CARRY_ROWS, 2 * d_ff), F32)]
    kern = functools.partial(_outffn_kernel, tm=tm, d_ff=d_ff, fc=d_ff // 2, seq_rows=seq_rows,
                             tiles_per_batch=tpb, has_state=has_state)
    return pl.pallas_call(
        kern, grid=(tiles,), in_specs=in_specs, out_specs=out_specs, out_shape=out_shape,
        scratch_shapes=scratch,
        compiler_params=_cparams(("arbitrary",)),
        name="outffn_sample" if has_state else "outffn_prompt",
    )(*args)


def _layer(l, yp, ys, c_prompt, c_sample, cache_k, cache_v, state_conv, page_table,
           norm1_g, norm2_g, w_ada, b_ada, w_in, q_norm_g, k_norm_g, sb_bias, v_norm_g,
           w_s, b_s, w_out, w_up, conv_w, conv_b, w_down):
    bp, tp, d = yp.shape
    db, ts, _ = ys.shape
    d_ff = w_down.shape[1]
    n_heads = sb_bias.shape[1]
    d_sb = n_heads * HEAD_DIM
    n_grp = v_norm_g.shape[1]
    d_mlp = n_grp * CHUNK
    n_pool, page = cache_k.shape[1], cache_k.shape[2]

    n_c = bp + db
    pad = (-n_c) % 8
    c_all = jnp.concatenate([c_prompt, c_sample, jnp.zeros((pad, d), F32)], axis=0)
    mod = _ada(c_all, w_ada[l], b_ada[l])
    mod_p = mod[:bp].reshape(bp, N_MOD, 1, d)
    mod_s = jnp.repeat(mod[bp:n_c].reshape(db, N_MOD, d).transpose(1, 0, 2), ts, axis=1)

    win_b = w_in[l].astype(BF16)
    wnat_b = jnp.concatenate([win_b[:, :d_sb], win_b[:, 2 * d_sb:]], axis=1)
    wkvt_b = win_b[:, d_sb:3 * d_sb].T
    wout_b = w_out[l].astype(BF16)
    wup_b = w_up[l].astype(BF16)
    wdn_b = w_down[l].astype(BF16)
    g1 = norm1_g[l].reshape(1, d)
    n2 = norm2_g[l].reshape(1, d)
    gq = (jnp.tile(q_norm_g[l], n_heads) * (LOG2E * HEAD_DIM ** -0.5)).reshape(1, d_sb)
    sbb2 = sb_bias[l] * LOG2E
    gk = jnp.tile(k_norm_g[l], n_heads).reshape(1, d_sb)
    gkc = jnp.broadcast_to(k_norm_g[l][:, None], (HEAD_DIM, LANES))
    gv = v_norm_g[l].reshape(1, d_mlp)
    qaux = jnp.where((jnp.arange(LANES) == HEAD_DIM)[None, :], sbb2[:, None], 0.0).astype(F32)
    cb = conv_b[l].reshape(1, 2 * d_ff)

    bmix_p = jnp.repeat(b_s[l].T, CHUNK, axis=1)
    xp2 = yp.reshape(bp * tp, d)
    kt, vt, qh, kh, vb, oml_p = _inproj_prompt(
        xp2, mod_p, bp, ATTN_BLK, wnat_b, wkvt_b, g1, gq, gkc, gv, qaux, w_s[l], bmix_p,
        d_sb, d_mlp)
    osb_p = _attn_prompt(qh, kh, vb.reshape(bp, tp, d_sb)).reshape(bp * tp, d_sb)
    yp2, st_p = _outffn(xp2, osb_p, oml_p, mod_p, False, bp, tp, 512, n2, wout_b, wup_b,
                        conv_w[l], cb, wdn_b, None)

    rows_s = db * ts
    wmix_s = jnp.tile(w_s[l][:, :ts, :ts], (1, CHUNK // ts, CHUNK // ts))
    bmix_s = jnp.repeat(jnp.tile(b_s[l][:, :ts], (1, CHUNK // ts)).T, CHUNK, axis=1)
    xs2 = ys.reshape(rows_s, d)
    ks_, vs_, qn_s, gv_s, oml_s = _inproj_sample(
        xs2, mod_s, ts, win_b, g1, gq, gk, gv, wmix_s, bmix_s, d_sb, d_mlp)
    bias_rows = jnp.tile(sbb2, ts).reshape(ts * n_heads, 1)
    cache_kt = cache_k[l].transpose(0, 2, 3, 1).reshape(n_pool, d_sb, page)
    cache_vt = cache_v[l].transpose(0, 2, 3, 1).reshape(n_pool, d_sb, page)
    osb_s = _attn_sample(page_table, qn_s.reshape(db, ts, d_sb), ks_.reshape(db, ts, d_sb),
                         vs_.reshape(db, ts, d_sb), bias_rows, cache_kt, cache_vt, n_heads)
    st = state_conv[l]
    p1 = jnp.concatenate([st[:, 1:2], jnp.zeros((db, ts - 1, 2 * d_ff), F32)], axis=1)
    p2 = jnp.concatenate([st, jnp.zeros((db, ts - 2, 2 * d_ff), F32)], axis=1)
    ys2, up_s = _outffn(xs2, osb_s.reshape(rows_s, d_sb), oml_s, mod_s, True, db, ts, rows_s, n2,
                        wout_b, wup_b, conv_w[l], cb, wdn_b,
                        (p1.reshape(rows_s, 2 * d_ff), p2.reshape(rows_s, 2 * d_ff)))

    outs = (kt.reshape(bp, n_heads, HEAD_DIM, tp).transpose(0, 3, 1, 2),
            vt.reshape(bp, n_heads, HEAD_DIM, tp).transpose(0, 3, 1, 2),
            ks_.reshape(db, ts, n_heads, HEAD_DIM), vs_.reshape(db, ts, n_heads, HEAD_DIM),
            gv_s.reshape(db, ts, n_grp, CHUNK),
            st_p[:, CARRY_ROWS - (CONV_W - 1):],
            up_s.reshape(db, ts, 2 * d_ff)[:, ts - (CONV_W - 1):])
    return yp2.reshape(bp, tp, d), ys2.reshape(db, ts, d), outs


def kernel(x_prompt, x_sample, c_prompt, c_sample, cache_k, cache_v, state_conv, page_table,
           norm1_g, norm2_g, w_ada, b_ada, w_in, q_norm_g, k_norm_g, sb_bias, v_norm_g,
           w_s, b_s, w_out, w_up, conv_w, conv_b, w_down):
    depth = w_in.shape[0]
    yp, ys = x_prompt, x_sample
    per_layer = []
    for l in range(depth):
        yp, ys, outs = _layer(l, yp, ys, c_prompt, c_sample, cache_k, cache_v, state_conv,
                              page_table, norm1_g, norm2_g, w_ada, b_ada, w_in, q_norm_g,
                              k_norm_g, sb_bias, v_norm_g, w_s, b_s, w_out, w_up, conv_w,
                              conv_b, w_down)
        per_layer.append(outs)
    stacked = tuple(jnp.stack([o[i] for o in per_layer]) for i in range(7))
    return (yp, ys) + stacked
```

```python
import functools

import jax
import jax.numpy as jnp
from jax import lax
from jax.experimental import pallas as pl
from jax.experimental.pallas import tpu as pltpu

F32 = jnp.float32
BF16 = jnp.bfloat16

EPS = 1e-6
LOG2E = 1.4426950408889634
HEAD_DIM = 64
LANES = 128
CHUNK = 128
N_MOD = 6
CONV_W = 3
CARRY_ROWS = 8
ATTN_BLK = 256
SAMPLE_CHUNK_PAGES = 16
SAMPLE_RING_SLOTS = 3

VMEM_LIMIT = 56 * 1024 * 1024

NT_DIMS = (((1,), (1,)), ((), ()))


def _cparams(sem):
    return pltpu.CompilerParams(dimension_semantics=sem, vmem_limit_bytes=VMEM_LIMIT)


def _resident(shape):
    nd = len(shape)
    return pl.BlockSpec(shape, lambda *_: (0,) * nd, pipeline_mode=pl.Buffered(1))


def _mod_spec(per_row, k, tm, d, tiles_per_batch):
    if per_row:
        return pl.BlockSpec((None, tm, d), lambda i: (k, i, 0))
    return pl.BlockSpec((None, None, 1, d), lambda i: (i // tiles_per_batch, k, 0, 0))


def _ada_kernel(c_ref, w_ref, b_ref, o_ref):
    c = c_ref[...]
    s = c * jax.nn.sigmoid(c)
    o_ref[...] = jnp.dot(s, w_ref[...], preferred_element_type=F32,
                         precision=lax.Precision.HIGHEST) + b_ref[...]


def _ada(c_all, w_ada, b_ada):
    rows, d = c_all.shape
    n = w_ada.shape[1]
    bn = n // 4
    return pl.pallas_call(
        _ada_kernel,
        grid=(n // bn,),
        in_specs=[pl.BlockSpec((rows, d), lambda j: (0, 0)),
                  pl.BlockSpec((d, bn), lambda j: (0, j)),
                  pl.BlockSpec((1, bn), lambda j: (0, j))],
        out_specs=pl.BlockSpec((rows, bn), lambda j: (0, j)),
        out_shape=jax.ShapeDtypeStruct((rows, n), F32),
        compiler_params=_cparams(("arbitrary",)),
        name="ada",
    )(c_all, w_ada, b_ada.reshape(1, n))


def _two_head_norm(blk, g):
    lo = lax.broadcasted_iota(jnp.int32, blk.shape, 1) < HEAD_DIM
    sq = blk * blk
    s_lo = jnp.sum(jnp.where(lo, sq, 0.0), axis=-1, keepdims=True)
    s_hi = jnp.sum(jnp.where(lo, 0.0, sq), axis=-1, keepdims=True)
    r = lax.rsqrt(jnp.where(lo, s_lo, s_hi) * (1.0 / HEAD_DIM) + EPS)
    return blk * r * g


def _spatial_gate(proj, u0, g0, gv_ref, wmix_ref, bmix_ref, om_ref, gvo_ref, tm, n_grp, seq_rows):
    r_i = lax.broadcasted_iota(jnp.int32, (CHUNK, CHUNK), 0)
    c_i = lax.broadcasted_iota(jnp.int32, (CHUNK, CHUNK), 1)
    if seq_rows >= CHUNK:
        keep = r_i >= c_i
    else:
        keep = jnp.logical_and(r_i >= c_i, c_i >= (r_i // seq_rows) * seq_rows)
    for g in range(n_grp):
        cs = slice(g * CHUNK, (g + 1) * CHUNK)
        wm = jnp.where(keep, wmix_ref[g], 0.0).astype(BF16)
        u = jax.nn.gelu(proj[:, u0 + g * CHUNK:u0 + (g + 1) * CHUNK])
        vg = jax.nn.gelu(proj[:, g0 + g * CHUNK:g0 + (g + 1) * CHUNK])
        vn = vg * lax.rsqrt(jnp.mean(vg * vg, axis=-1, keepdims=True) + EPS) * gv_ref[:, cs]
        if gvo_ref is not None:
            gvo_ref[:, cs] = vn
        vnb = vn.astype(BF16)
        for c in range(tm // CHUNK):
            rs = slice(c * CHUNK, (c + 1) * CHUNK)
            mixed = jnp.dot(wm, vnb[rs], preferred_element_type=F32) + bmix_ref[:, cs]
            om_ref[rs, cs] = (u[rs] * mixed).astype(BF16)


def _normed_input(x_ref, sh_ref, sc_ref, g_ref):
    x = x_ref[...]
    ms = jnp.mean(x * x, axis=-1, keepdims=True)
    h = x * lax.rsqrt(ms + EPS) * g_ref[...]
    return (h * (1.0 + sc_ref[...]) + sh_ref[...]).astype(BF16)


def _inproj_prompt_kernel(x_ref, sh_ref, sc_ref, g1_ref, wnat_ref, wkvt_ref, gq_ref, gkc_ref,
                          gv_ref, qaux_ref, wmix_ref, bmix_ref,
                          kt_ref, vt_ref, qh_ref, kh_ref, vb_ref, om_ref,
                          *, tm, d_sb, d_mlp, seq_rows):
    n_heads = d_sb // HEAD_DIM
    hb = _normed_input(x_ref, sh_ref, sc_ref, g1_ref)
    proj = jnp.dot(hb, wnat_ref[...], preferred_element_type=F32)
    kvt = lax.dot_general(wkvt_ref[...], hb, NT_DIMS, preferred_element_type=F32)

    gkc = jnp.tile(gkc_ref[...], (1, tm // LANES))
    aux_rows = jnp.where(lax.broadcasted_iota(jnp.int32, (HEAD_DIM, tm), 0) == 0, 1.0, 0.0)
    for hd in range(n_heads):
        rs = slice(hd * HEAD_DIM, (hd + 1) * HEAD_DIM)
        blk = kvt[rs, :]
        kn = blk * lax.rsqrt(jnp.mean(blk * blk, axis=0, keepdims=True) + EPS) * gkc
        kt_ref[rs, :] = kn
        kh_ref[hd, 0:HEAD_DIM, :] = kn.astype(BF16)
        kh_ref[hd, HEAD_DIM:, :] = aux_rows.astype(BF16)
    vt_ref[...] = kvt[d_sb:, :]

    lo = lax.broadcasted_iota(jnp.int32, (tm, LANES), 1) < HEAD_DIM
    for p in range(d_sb // LANES):
        cs = slice(p * LANES, (p + 1) * LANES)
        qn = _two_head_norm(proj[:, p * LANES:(p + 1) * LANES], gq_ref[:, cs])
        vb_ref[:, cs] = proj[:, d_sb + p * LANES:d_sb + (p + 1) * LANES].astype(BF16)
        qh_ref[2 * p] = jnp.where(lo, qn, qaux_ref[2 * p:2 * p + 1, :]).astype(BF16)
        qr = pltpu.roll(qn, HEAD_DIM, axis=1)
        qh_ref[2 * p + 1] = jnp.where(lo, qr, qaux_ref[2 * p + 1:2 * p + 2, :]).astype(BF16)

    _spatial_gate(proj, 2 * d_sb, 2 * d_sb + d_mlp, gv_ref, wmix_ref, bmix_ref, om_ref, None,
                  tm, d_mlp // CHUNK, seq_rows)


def _inproj_sample_kernel(x_ref, sh_ref, sc_ref, g1_ref, win_ref, gq_ref, gk_ref, gv_ref,
                          wmix_ref, bmix_ref, k_ref, v_ref, qn_ref, gvo_ref, om_ref,
                          *, tm, d_sb, d_mlp, seq_rows):
    hb = _normed_input(x_ref, sh_ref, sc_ref, g1_ref)
    proj = jnp.dot(hb, win_ref[...], preferred_element_type=F32)
    for p in range(d_sb // LANES):
        cs = slice(p * LANES, (p + 1) * LANES)
        qn_ref[:, cs] = _two_head_norm(proj[:, p * LANES:(p + 1) * LANES], gq_ref[:, cs])
        k_ref[:, cs] = _two_head_norm(proj[:, d_sb + p * LANES:d_sb + (p + 1) * LANES],
                                      gk_ref[:, cs])
        v_ref[:, cs] = proj[:, 2 * d_sb + p * LANES:2 * d_sb + (p + 1) * LANES]
    _spatial_gate(proj, 3 * d_sb, 3 * d_sb + d_mlp, gv_ref, wmix_ref, bmix_ref, om_ref, gvo_ref,
                  tm, d_mlp // CHUNK, seq_rows)


def _inproj_prompt(x2, mod, n_batch, tm, wnat_b, wkvt_b, g1, gq, gkc, gv, qaux, wmix, bmix,
                   d_sb, d_mlp):
    rows, d = x2.shape
    n_heads = d_sb // HEAD_DIM
    t_len = rows // n_batch
    tpb = t_len // tm
    const = lambda i: (0, 0)
    in_specs = [
        pl.BlockSpec((tm, d), lambda i: (i, 0)),
        _mod_spec(False, 0, tm, d, tpb),
        _mod_spec(False, 1, tm, d, tpb),
        pl.BlockSpec((1, d), const),
        _resident(wnat_b.shape),
        _resident(wkvt_b.shape),
        pl.BlockSpec((1, d_sb), const),
        pl.BlockSpec((HEAD_DIM, LANES), const),
        pl.BlockSpec((1, d_mlp), const),
        pl.BlockSpec((n_heads, LANES), const),
        pl.BlockSpec(wmix.shape, lambda i: (0, 0, 0)),
        pl.BlockSpec((CHUNK, d_mlp), const),
    ]
    row_spec = lambda w: pl.BlockSpec((tm, w), lambda i: (i, 0))
    tmin_spec = pl.BlockSpec((None, d_sb, tm), lambda i: (i // tpb, 0, i % tpb))
    out_specs = [
        tmin_spec, tmin_spec,
        pl.BlockSpec((None, n_heads, tm, LANES), lambda i: (i // tpb, 0, i % tpb, 0)),
        pl.BlockSpec((None, n_heads, None, LANES, tm), lambda i: (i // tpb, 0, i % tpb, 0, 0)),
        row_spec(d_sb), row_spec(d_mlp)]
    out_shape = [
        jax.ShapeDtypeStruct((n_batch, d_sb, t_len), F32),
        jax.ShapeDtypeStruct((n_batch, d_sb, t_len), F32),
        jax.ShapeDtypeStruct((n_batch, n_heads, t_len, LANES), BF16),
        jax.ShapeDtypeStruct((n_batch, n_heads, tpb, LANES, tm), BF16),
        jax.ShapeDtypeStruct((rows, d_sb), BF16),
        jax.ShapeDtypeStruct((rows, d_mlp), BF16)]
    kern = functools.partial(_inproj_prompt_kernel, tm=tm, d_sb=d_sb, d_mlp=d_mlp, seq_rows=t_len)
    return pl.pallas_call(
        kern, grid=(rows // tm,), in_specs=in_specs, out_specs=out_specs, out_shape=out_shape,
        compiler_params=_cparams(("arbitrary",)), name="inproj_prompt",
    )(x2, mod, mod, g1, wnat_b, wkvt_b, gq, gkc, gv, qaux, wmix, bmix)


def _inproj_sample(x2, mod, seq_rows, win_b, g1, gq, gk, gv, wmix, bmix, d_sb, d_mlp):
    rows, d = x2.shape
    tm = rows
    const = lambda i: (0, 0)
    in_specs = [
        pl.BlockSpec((tm, d), lambda i: (i, 0)),
        _mod_spec(True, 0, tm, d, 1),
        _mod_spec(True, 1, tm, d, 1),
        pl.BlockSpec((1, d), const),
        _resident(win_b.shape),
        pl.BlockSpec((1, d_sb), const),
        pl.BlockSpec((1, d_sb), const),
        pl.BlockSpec((1, d_mlp), const),
        pl.BlockSpec(wmix.shape, lambda i: (0, 0, 0)),
        pl.BlockSpec((CHUNK, d_mlp), const),
    ]
    row_spec = lambda w: pl.BlockSpec((tm, w), lambda i: (i, 0))
    out_specs = [row_spec(d_sb), row_spec(d_sb), row_spec(d_sb), row_spec(d_mlp), row_spec(d_mlp)]
    out_shape = [jax.ShapeDtypeStruct((rows, d_sb), F32),
                 jax.ShapeDtypeStruct((rows, d_sb), F32),
                 jax.ShapeDtypeStruct((rows, d_sb), F32),
                 jax.ShapeDtypeStruct((rows, d_mlp), F32),
                 jax.ShapeDtypeStruct((rows, d_mlp), BF16)]
    kern = functools.partial(_inproj_sample_kernel, tm=tm, d_sb=d_sb, d_mlp=d_mlp,
                             seq_rows=seq_rows)
    return pl.pallas_call(
        kern, grid=(1,), in_specs=in_specs, out_specs=out_specs, out_shape=out_shape,
        compiler_params=_cparams(("arbitrary",)), name="inproj_sample",
    )(x2, mod, mod, g1, win_b, gq, gk, gv, wmix, bmix)


def _softplus2(z2):
    return jnp.maximum(z2, 0.0) + jnp.log(1.0 + jnp.exp2(-jnp.abs(z2))) * LOG2E


def _strict_upper(n):
    r = lax.broadcasted_iota(jnp.int32, (n, n), 0)
    c = lax.broadcasted_iota(jnp.int32, (n, n), 1)
    return jnp.where(r > c, 1.0, 0.0).astype(BF16)


def _sample_attention(pt_ref, q_ref, kn_ref, vn_ref, bias_ref, ck_hbm, cv_hbm, o_ref,
                      kbuf, vbuf, sem, qbd_ref, acc_ref, acct_ref, car_ref,
                      *, n_seq, n_tok, n_heads, pages_per_chunk, n_chunks, n_buf, page, sub):
    total = n_seq * n_chunks
    d_sb = n_heads * HEAD_DIM
    rows = n_tok * n_heads
    ck = pages_per_chunk * page
    n_sub = ck // sub

    def chunk_copies(g):
        seq = g // n_chunks
        chunk = n_chunks - 1 - g % n_chunks
        slot = g % n_buf
        cps = []
        for pg in range(pages_per_chunk):
            pid = pt_ref[seq, chunk * pages_per_chunk + pg]
            dst = pl.ds(pg * page, page)
            cps.append(pltpu.make_async_copy(ck_hbm.at[pid], kbuf.at[slot, :, dst], sem.at[0, slot]))
            cps.append(pltpu.make_async_copy(cv_hbm.at[pid], vbuf.at[slot, :, dst], sem.at[1, slot]))
        return cps

    def prime():
        for g0 in range(n_buf - 1):
            for cp in chunk_copies(g0):
                cp.start()

    def own_mask():
        r_i = lax.broadcasted_iota(jnp.int32, (rows, d_sb), 0)
        c_i = lax.broadcasted_iota(jnp.int32, (rows, d_sb), 1)
        return (r_i % n_heads) == (c_i // HEAD_DIM)

    def begin_sequence(seq):
        q_nat = q_ref[seq]
        q_rep = jnp.concatenate(
            [jnp.broadcast_to(q_nat[t:t + 1, :], (n_heads, d_sb)) for t in range(n_tok)], axis=0)
        q_bd = jnp.where(own_mask(), q_rep, 0.0)
        qbd_ref[...] = q_bd.astype(BF16)
        bias = bias_ref[...]
        tok = lax.broadcasted_iota(jnp.int32, (rows, 1), 0) // n_heads
        acc = jnp.zeros((rows, d_sb), F32)
        car = jnp.zeros((rows, 1), F32)
        k_new = kn_ref[seq]
        v_new = vn_ref[seq]
        for s in range(n_tok - 1, -1, -1):
            z = jnp.sum(q_bd * k_new[s:s + 1, :], axis=-1, keepdims=True) + bias
            vis = tok > s
            sp = jnp.where(vis, _softplus2(z), 0.0)
            a = jnp.where(vis, jnp.exp2(z - sp - car), 0.0)
            acc = acc + a * v_new[s:s + 1, :]
            car = car + sp
        acc_ref[...] = acc
        acct_ref[...] = jnp.zeros_like(acct_ref)
        car_ref[...] = jnp.broadcast_to(car, (rows, LANES))

    def end_sequence(seq):
        res = jnp.where(own_mask(), acc_ref[...] + acct_ref[...].T[0:rows, :], 0.0)
        o_ref[seq] = jnp.concatenate(
            [jnp.sum(res[t * n_heads:(t + 1) * n_heads], axis=0, keepdims=True)
             for t in range(n_tok)], axis=0)

    def step(g):
        n = g % n_chunks
        seq = g // n_chunks

        @pl.when(n == 0)
        def _():
            begin_sequence(seq)

        slot = g % n_buf
        for cp in chunk_copies(g):
            cp.wait()

        @pl.when(g + n_buf - 1 < total)
        def _():
            for cp in chunk_copies(g + n_buf - 1):
                cp.start()

        upper = _strict_upper(sub)
        z_all = jnp.dot(qbd_ref[...], kbuf[slot].astype(BF16),
                        preferred_element_type=F32) + bias_ref[...]
        sps = [_softplus2(z_all[:, s * sub:(s + 1) * sub]) for s in range(n_sub)]
        sums = [jnp.sum(sp, axis=-1, keepdims=True) for sp in sps]
        cr = car_ref[...]
        a_parts = [None] * n_sub
        for s in range(n_sub - 1, -1, -1):
            later = jnp.dot(sps[s].astype(BF16), upper, preferred_element_type=F32)
            t = z_all[:, s * sub:(s + 1) * sub] - sps[s] - later - jnp.tile(cr, (1, sub // LANES))
            a_parts[s] = jnp.exp2(t).astype(BF16)
            cr = cr + sums[s]
        a_all = jnp.concatenate(
            [jnp.concatenate(a_parts, axis=1), jnp.zeros((LANES - rows, ck), BF16)], axis=0)
        acct_ref[...] += lax.dot_general(vbuf[slot].astype(BF16), a_all, NT_DIMS,
                                         preferred_element_type=F32)
        car_ref[...] = cr

        @pl.when(n == n_chunks - 1)
        def _():
            end_sequence(seq)

    return prime, step


def _attn_kernel(pt_ref, q_ref, k_ref, v_ref, qs_ref, kn_ref, vn_ref, bias_ref, ck_hbm, cv_hbm,
                 o_ref, os_ref,
                 acc_ref, car_ref, kbuf, vbuf, sem, qbd_ref, sacc_ref, sacct_ref, scar_ref, cnt_ref,
                 *, blk, n_h, units_per_chunk, sample_cfg):
    b = pl.program_id(0)
    i = pl.program_id(1)
    prime, sample_step = _sample_attention(
        pt_ref, qs_ref, kn_ref, vn_ref, bias_ref, ck_hbm, cv_hbm, os_ref,
        kbuf, vbuf, sem, qbd_ref, sacc_ref, sacct_ref, scar_ref, **sample_cfg)
    n_sample_steps = sample_cfg["n_seq"] * sample_cfg["n_chunks"]

    @pl.when(jnp.logical_and(b == 0, i == 0))
    def _():
        cnt_ref[0] = 0
        prime()

    def tick():
        u = cnt_ref[0]
        cnt_ref[0] = u + 1
        g = u // units_per_chunk

        @pl.when(jnp.logical_and(u % units_per_chunk == units_per_chunk - 1, g < n_sample_steps))
        def _():
            sample_step(g)

    upper = _strict_upper(blk)
    reps = blk // LANES
    row = lax.broadcasted_iota(jnp.int32, (blk, blk), 0)
    col = lax.broadcasted_iota(jnp.int32, (blk, blk), 1)
    causal = col < row

    def tiles(j, mask):
        heads = range(n_h)
        ks = pl.multiple_of(j * blk, blk)
        zs = [jnp.dot(q_ref[hh], k_ref[hh, j], preferred_element_type=F32) for hh in heads]
        sps = [_softplus2(z) for z in zs]
        if mask is not None:
            sps = [jnp.where(mask, sp, 0.0) for sp in sps]
        lbs = [zs[hh] - sps[hh] for hh in heads]
        later_all = jnp.dot(jnp.concatenate([sp.astype(BF16) for sp in sps], axis=0), upper,
                            preferred_element_type=F32)
        laters = [later_all[hh * blk:(hh + 1) * blk] for hh in heads]
        cars = [car_ref[hh] for hh in heads]
        for hh in heads:
            car_ref[hh] = cars[hh] + jnp.sum(sps[hh], axis=-1, keepdims=True)
        a_s = [jnp.exp2(lbs[hh] - laters[hh] - jnp.tile(cars[hh], (1, reps))) for hh in heads]
        if mask is not None:
            a_s = [jnp.where(mask, a, 0.0) for a in a_s]
        for p in range(n_h // 2):
            v = v_ref[pl.ds(ks, blk), p * LANES:(p + 1) * LANES]
            a_pair = jnp.concatenate([a_s[2 * p].astype(BF16), a_s[2 * p + 1].astype(BF16)], axis=0)
            av = jnp.dot(a_pair, v, preferred_element_type=F32)
            acc_ref[2 * p] += av[:blk]
            acc_ref[2 * p + 1] += av[blk:]

    acc_ref[...] = jnp.zeros_like(acc_ref)
    car_ref[...] = jnp.zeros_like(car_ref)
    tiles(i, causal)
    tick()

    def body(n, c):
        tiles(i - 1 - n, None)
        tick()
        return c

    lax.fori_loop(0, i, body, 0)

    lo = lax.broadcasted_iota(jnp.int32, (blk, LANES), 1) < HEAD_DIM
    for p in range(n_h // 2):
        o_ref[:, p * LANES:(p + 1) * LANES] = jnp.where(
            lo, acc_ref[2 * p], acc_ref[2 * p + 1]).astype(o_ref.dtype)


def _attn(qh, kh, vb, page_table, q_nat, k_new, v_new, bias_rows, cache_kt, cache_vt):
    n_batch, n_heads, t_len, _ = qh.shape
    blk = kh.shape[-1]
    n_blk = t_len // blk
    d_sb = n_heads * HEAD_DIM
    n_seq, n_tok, _ = q_nat.shape
    n_pages = page_table.shape[1]
    page = cache_kt.shape[2]
    pages_per_chunk = SAMPLE_CHUNK_PAGES
    n_buf = SAMPLE_RING_SLOTS
    n_chunks = n_pages // pages_per_chunk
    assert n_chunks * pages_per_chunk == n_pages
    rows = n_tok * n_heads
    ck = pages_per_chunk * page
    units = n_batch * n_blk * (n_blk + 1) // 2
    units_per_chunk = units // (n_seq * n_chunks)
    assert units_per_chunk >= 1, "not enough prompt work to pace the sample chunks"
    sample_cfg = dict(n_seq=n_seq, n_tok=n_tok, n_heads=n_heads, pages_per_chunk=pages_per_chunk,
                      n_chunks=n_chunks, n_buf=n_buf, page=page, sub=blk)
    whole = lambda shape: pl.BlockSpec(shape, lambda b, i, pt: (0,) * len(shape))
    grid_spec = pltpu.PrefetchScalarGridSpec(
        num_scalar_prefetch=1,
        grid=(n_batch, n_blk),
        in_specs=[pl.BlockSpec((None, n_heads, blk, LANES), lambda b, i, pt: (b, 0, i, 0)),
                  pl.BlockSpec((None, n_heads, n_blk, LANES, blk), lambda b, i, pt: (b, 0, 0, 0, 0),
                               pipeline_mode=pl.Buffered(1)),
                  pl.BlockSpec((None, t_len, d_sb), lambda b, i, pt: (b, 0, 0),
                               pipeline_mode=pl.Buffered(1)),
                  whole((n_seq, n_tok, d_sb)), whole((n_seq, n_tok, d_sb)),
                  whole((n_seq, n_tok, d_sb)), whole((rows, 1)),
                  pl.BlockSpec(memory_space=pl.ANY),
                  pl.BlockSpec(memory_space=pl.ANY)],
        out_specs=[pl.BlockSpec((None, blk, d_sb), lambda b, i, pt: (b, i, 0)),
                   whole((n_seq, n_tok, d_sb))],
        scratch_shapes=[pltpu.VMEM((n_heads, blk, LANES), F32),
                        pltpu.VMEM((n_heads, blk, LANES), F32),
                        pltpu.VMEM((n_buf, d_sb, ck), F32),
                        pltpu.VMEM((n_buf, d_sb, ck), F32),
                        pltpu.SemaphoreType.DMA((2, n_buf)),
                        pltpu.VMEM((rows, d_sb), BF16),
                        pltpu.VMEM((rows, d_sb), F32),
                        pltpu.VMEM((d_sb, LANES), F32),
                        pltpu.VMEM((rows, LANES), F32),
                        pltpu.SMEM((1,), jnp.int32)],
    )
    kern = functools.partial(_attn_kernel, blk=blk, n_h=n_heads, units_per_chunk=units_per_chunk,
                             sample_cfg=sample_cfg)
    return pl.pallas_call(
        kern, grid_spec=grid_spec,
        out_shape=[jax.ShapeDtypeStruct((n_batch, t_len, d_sb), BF16),
                   jax.ShapeDtypeStruct((n_seq, n_tok, d_sb), F32)],
        compiler_params=_cparams(("arbitrary", "arbitrary")),
        name="attn",
    )(page_table, qh, kh, vb, q_nat, k_new, v_new, bias_rows, cache_kt, cache_vt)


def _outffn_kernel(*refs, tm, d_ff, fc, seq_rows, tiles_per_batch, has_state):
    if has_state:
        (x_ref, osb_ref, oml_ref, g1_ref, sh_ref, sc_ref, g2_ref, n2_ref, wout_ref, wup_ref,
         cw_ref, cb_ref, wdn_ref, p1_ref, p2_ref, y_ref, up_ref) = refs
    else:
        (x_ref, osb_ref, oml_ref, g1_ref, sh_ref, sc_ref, g2_ref, n2_ref, wout_ref, wup_ref,
         cw_ref, cb_ref, wdn_ref, y_ref, st_ref, carry_ref) = refs
    d_sb = osb_ref.shape[1]
    mix = (jnp.dot(osb_ref[...].astype(BF16), wout_ref[0:d_sb, :], preferred_element_type=F32)
           + jnp.dot(oml_ref[...], wout_ref[d_sb:, :], preferred_element_type=F32))
    x1 = x_ref[...] + g1_ref[...] * mix
    ms = jnp.mean(x1 * x1, axis=-1, keepdims=True)
    h2 = x1 * lax.rsqrt(ms + EPS) * n2_ref[...]
    h2 = (h2 * (1.0 + sc_ref[...]) + sh_ref[...]).astype(BF16)

    row = lax.broadcasted_iota(jnp.int32, (tm, 1), 0)
    if has_state:
        t_in = row % seq_rows
    else:
        t_in = row
        first = (pl.program_id(0) % tiles_per_batch) == 0

        @pl.when(first)
        def _():
            carry_ref[...] = jnp.zeros_like(carry_ref)

    def conv(u, cols):
        r1 = pltpu.roll(u, 1, axis=0)
        r2 = pltpu.roll(u, 2, axis=0)
        if has_state:
            up_ref[:, cols] = u
            prev1 = jnp.where(t_in >= 1, r1, p1_ref[:, cols])
            prev2 = jnp.where(t_in >= 2, r2, p2_ref[:, cols])
        else:
            c6 = carry_ref[CARRY_ROWS - 2:CARRY_ROWS - 1, cols]
            c7 = carry_ref[CARRY_ROWS - 1:CARRY_ROWS, cols]
            prev1 = jnp.where(t_in >= 1, r1, c7)
            prev2 = jnp.where(t_in >= 2, r2, jnp.where(t_in == 0, c6, c7))
            carry_ref[:, cols] = u[tm - CARRY_ROWS:, :]
        return (cb_ref[:, cols] + cw_ref[0:1, cols] * prev2 + cw_ref[1:2, cols] * prev1
                + cw_ref[2:3, cols] * u)

    acc = jnp.zeros((tm, x1.shape[1]), F32)
    for f in range(d_ff // fc):
        gc = slice(f * fc, (f + 1) * fc)
        vc = slice(d_ff + f * fc, d_ff + (f + 1) * fc)
        cg = conv(jnp.dot(h2, wup_ref[:, gc], preferred_element_type=F32), gc)
        cv = conv(jnp.dot(h2, wup_ref[:, vc], preferred_element_type=F32), vc)
        act = (cg * jax.nn.sigmoid(cg) * cv).astype(BF16)
        acc = acc + jnp.dot(act, wdn_ref[gc, :], preferred_element_type=F32)
    y_ref[...] = x1 + g2_ref[...] * acc

    if not has_state:
        @pl.when((pl.program_id(0) % tiles_per_batch) == tiles_per_batch - 1)
        def _():
            st_ref[...] = carry_ref[...]


def _outffn(x2, osb, oml, mod, per_row, n_batch, seq_rows, tm, n2, wout_b, wup_b, conv_w, conv_b,
            wdn_b, state_rows):
    rows, d = x2.shape
    d_ff = wdn_b.shape[0]
    d_sb = osb.shape[1]
    d_mlp = oml.shape[1]
    tiles = rows // tm
    tpb = max(tiles // n_batch, 1)
    has_state = state_rows is not None
    const = lambda i: (0, 0)
    row_spec = lambda w: pl.BlockSpec((tm, w), lambda i: (i, 0))
    in_specs = [
        row_spec(d), row_spec(d_sb), row_spec(d_mlp),
        _mod_spec(per_row, 2, tm, d, tpb),
        _mod_spec(per_row, 3, tm, d, tpb),
        _mod_spec(per_row, 4, tm, d, tpb),
        _mod_spec(per_row, 5, tm, d, tpb),
        pl.BlockSpec((1, d), const),
        _resident(wout_b.shape),
        _resident(wup_b.shape),
        pl.BlockSpec(conv_w.shape, const),
        pl.BlockSpec((1, 2 * d_ff), const),
        _resident(wdn_b.shape),
    ]
    args = [x2, osb, oml, mod, mod, mod, mod, n2, wout_b, wup_b, conv_w, conv_b, wdn_b]
    if has_state:
        in_specs += [_resident((rows, 2 * d_ff)), _resident((rows, 2 * d_ff))]
        args += list(state_rows)
        out_specs = [row_spec(d), row_spec(2 * d_ff)]
        out_shape = [jax.ShapeDtypeStruct((rows, d), F32),
                     jax.ShapeDtypeStruct((rows, 2 * d_ff), F32)]
        scratch = []
    else:
        out_specs = [row_spec(d),
                     pl.BlockSpec((None, CARRY_ROWS, 2 * d_ff), lambda i: (i // tpb, 0, 0))]
        out_shape = [jax.ShapeDtypeStruct((rows, d), F32),
                     jax.ShapeDtypeStruct((n_batch, CARRY_ROWS, 2 * d_ff), F32)]
        scratch = [pltpu.VMEM((CARRY_ROWS, 2 * d_ff), F32)]
    kern = functools.partial(_outffn_kernel, tm=tm, d_ff=d_ff, fc=d_ff // 2, seq_rows=seq_rows,
                             tiles_per_batch=tpb, has_state=has_state)
    return pl.pallas_call(
        kern, grid=(tiles,), in_specs=in_specs, out_specs=out_specs, out_shape=out_shape,
        scratch_shapes=scratch,
        compiler_params=_cparams(("arbitrary",)),
        name="outffn_sample" if has_state else "outffn_prompt",
    )(*args)


def _layer(l, yp, ys, c_prompt, c_sample, cache_k, cache_v, state_conv, page_table,
           norm1_g, norm2_g, w_ada, b_ada, w_in, q_norm_g, k_norm_g, sb_bias, v_norm_g,
           w_s, b_s, w_out, w_up, conv_w, conv_b, w_down):
    bp, tp, d = yp.shape
    db, ts, _ = ys.shape
    d_ff = w_down.shape[1]
    n_heads = sb_bias.shape[1]
    d_sb = n_heads * HEAD_DIM
    n_grp = v_norm_g.shape[1]
    d_mlp = n_grp * CHUNK
    n_pool, page = cache_k.shape[1], cache_k.shape[2]

    n_c = bp + db
    pad = (-n_c) % 8
    c_all = jnp.concatenate([c_prompt, c_sample, jnp.zeros((pad, d), F32)], axis=0)
    mod = _ada(c_all, w_ada[l], b_ada[l])
    mod_p = mod[:bp].reshape(bp, N_MOD, 1, d)
    mod_s = jnp.repeat(mod[bp:n_c].reshape(db, N_MOD, d).transpose(1, 0, 2), ts, axis=1)

    win_b = w_in[l].astype(BF16)
    wnat_b = jnp.concatenate([win_b[:, :d_sb], win_b[:, 2 * d_sb:]], axis=1)
    wkvt_b = win_b[:, d_sb:3 * d_sb].T
    wout_b = w_out[l].astype(BF16)
    wup_b = w_up[l].astype(BF16)
    wdn_b = w_down[l].astype(BF16)
    g1 = norm1_g[l].reshape(1, d)
    n2 = norm2_g[l].reshape(1, d)
    gq = (jnp.tile(q_norm_g[l], n_heads) * (LOG2E * HEAD_DIM ** -0.5)).reshape(1, d_sb)
    sbb2 = sb_bias[l] * LOG2E
    gk = jnp.tile(k_norm_g[l], n_heads).reshape(1, d_sb)
    gkc = jnp.broadcast_to(k_norm_g[l][:, None], (HEAD_DIM, LANES))
    gv = v_norm_g[l].reshape(1, d_mlp)
    qaux = jnp.where((jnp.arange(LANES) == HEAD_DIM)[None, :], sbb2[:, None], 0.0).astype(F32)
    cb = conv_b[l].reshape(1, 2 * d_ff)

    bmix_p = jnp.repeat(b_s[l].T, CHUNK, axis=1)
    xp2 = yp.reshape(bp * tp, d)
    kt, vt, qh, kh, vb, oml_p = _inproj_prompt(
        xp2, mod_p, bp, ATTN_BLK, wnat_b, wkvt_b, g1, gq, gkc, gv, qaux, w_s[l], bmix_p,
        d_sb, d_mlp)

    rows_s = db * ts
    wmix_s = jnp.tile(w_s[l][:, :ts, :ts], (1, CHUNK // ts, CHUNK // ts))
    bmix_s = jnp.repeat(jnp.tile(b_s[l][:, :ts], (1, CHUNK // ts)).T, CHUNK, axis=1)
    xs2 = ys.reshape(rows_s, d)
    ks_, vs_, qn_s, gv_s, oml_s = _inproj_sample(
        xs2, mod_s, ts, win_b, g1, gq, gk, gv, wmix_s, bmix_s, d_sb, d_mlp)
    bias_rows = jnp.tile(sbb2, ts).reshape(ts * n_heads, 1)
    cache_kt = cache_k[l].transpose(0, 2, 3, 1).reshape(n_pool, d_sb, page)
    cache_vt = cache_v[l].transpose(0, 2, 3, 1).reshape(n_pool, d_sb, page)
    osb_p, osb_s = _attn(qh, kh, vb.reshape(bp, tp, d_sb), page_table,
                         qn_s.reshape(db, ts, d_sb), ks_.reshape(db, ts, d_sb),
                         vs_.reshape(db, ts, d_sb), bias_rows, cache_kt, cache_vt)
    yp2, st_p = _outffn(xp2, osb_p.reshape(bp * tp, d_sb), oml_p, mod_p, False, bp, tp, 512, n2,
                        wout_b, wup_b, conv_w[l], cb, wdn_b, None)
    st = state_conv[l]
    p1 = jnp.concatenate([st[:, 1:2], jnp.zeros((db, ts - 1, 2 * d_ff), F32)], axis=1)
    p2 = jnp.concatenate([st, jnp.zeros((db, ts - 2, 2 * d_ff), F32)], axis=1)
    ys2, up_s = _outffn(xs2, osb_s.reshape(rows_s, d_sb), oml_s, mod_s, True, db, ts, rows_s, n2,
                        wout_b, wup_b, conv_w[l], cb, wdn_b,
                        (p1.reshape(rows_s, 2 * d_ff), p2.reshape(rows_s, 2 * d_ff)))

    outs = (kt.reshape(bp, n_heads, HEAD_DIM, tp).transpose(0, 3, 1, 2),
            vt.reshape(bp, n_heads, HEAD_DIM, tp).transpose(0, 3, 1, 2),
            ks_.reshape(db, ts, n_heads, HEAD_DIM), vs_.reshape(db, ts, n_heads, HEAD_DIM),
            gv_s.reshape(db, ts, n_grp, CHUNK),
            st_p[:, CARRY_ROWS - (CONV_W - 1):],
            up_s.reshape(db, ts, 2 * d_ff)[:, ts - (CONV_W - 1):])
    return yp2.reshape(bp, tp, d), ys2.reshape(db, ts, d), outs


def kernel(x_prompt, x_sample, c_prompt, c_sample, cache_k, cache_v, state_conv, page_table,
           norm1_g, norm2_g, w_ada, b_ada, w_in, q_norm_g, k_norm_g, sb_bias, v_norm_g,
           w_s, b_s, w_out, w_up, conv_w, conv_b, w_down):
    depth = w_in.shape[0]
    yp, ys = x_prompt, x_sample
    per_layer = []
    for l in range(depth):
        yp, ys, outs = _layer(l, yp, ys, c_prompt, c_sample, cache_k, cache_v, state_conv,
                              page_table, norm1_g, norm2_g, w_ada, b_ada, w_in, q_norm_g,
                              k_norm_g, sb_bias, v_norm_g, w_s, b_s, w_out, w_up, conv_w,
                              conv_b, w_down)
        per_layer.append(outs)
    stacked = tuple(jnp.stack([o[i] for o in per_layer]) for i in range(7))
    return (yp, ys) + stacked
```

```python
import functools

import jax
import jax.numpy as jnp
from jax import lax
from jax.experimental import pallas as pl
from jax.experimental.pallas import tpu as pltpu

F32 = jnp.float32
BF16 = jnp.bfloat16

EPS = 1e-6
LOG2E = 1.4426950408889634
HEAD_DIM = 64
LANES = 128
CHUNK = 128
N_MOD = 6
CONV_W = 3
CARRY_ROWS = 8
ATTN_BLK = 256
SAMPLE_CHUNK_PAGES = 16
SAMPLE_RING_SLOTS = 3

VMEM_LIMIT = 56 * 1024 * 1024

NT_DIMS = (((1,), (1,)), ((), ()))


def _cparams(sem):
    return pltpu.CompilerParams(dimension_semantics=sem, vmem_limit_bytes=VMEM_LIMIT)


def _resident(shape):
    nd = len(shape)
    return pl.BlockSpec(shape, lambda *_: (0,) * nd, pipeline_mode=pl.Buffered(1))


def _mod_spec(per_row, k, tm, d, tiles_per_batch):
    if per_row:
        return pl.BlockSpec((None, tm, d), lambda i: (k, i, 0))
    return pl.BlockSpec((None, None, 1, d), lambda i: (i // tiles_per_batch, k, 0, 0))


def _ada_kernel(c_ref, w_ref, b_ref, o_ref):
    c = c_ref[...]
    s = c * jax.nn.sigmoid(c)
    o_ref[...] = jnp.dot(s, w_ref[...], preferred_element_type=F32,
                         precision=lax.Precision.HIGHEST) + b_ref[...]


def _ada(c_all, w_ada, b_ada):
    rows, d = c_all.shape
    n = w_ada.shape[1]
    bn = n // 4
    return pl.pallas_call(
        _ada_kernel,
        grid=(n // bn,),
        in_specs=[pl.BlockSpec((rows, d), lambda j: (0, 0)),
                  pl.BlockSpec((d, bn), lambda j: (0, j)),
                  pl.BlockSpec((1, bn), lambda j: (0, j))],
        out_specs=pl.BlockSpec((rows, bn), lambda j: (0, j)),
        out_shape=jax.ShapeDtypeStruct((rows, n), F32),
        compiler_params=_cparams(("arbitrary",)),
        name="ada",
    )(c_all, w_ada, b_ada.reshape(1, n))


def _two_head_norm(blk, g):
    lo = lax.broadcasted_iota(jnp.int32, blk.shape, 1) < HEAD_DIM
    sq = blk * blk
    s_lo = jnp.sum(jnp.where(lo, sq, 0.0), axis=-1, keepdims=True)
    s_hi = jnp.sum(jnp.where(lo, 0.0, sq), axis=-1, keepdims=True)
    r = lax.rsqrt(jnp.where(lo, s_lo, s_hi) * (1.0 / HEAD_DIM) + EPS)
    return blk * r * g


def _spatial_gate(proj, u0, g0, gv_ref, wmix_ref, bmix_ref, om_ref, gvo_ref, tm, n_grp, seq_rows):
    r_i = lax.broadcasted_iota(jnp.int32, (CHUNK, CHUNK), 0)
    c_i = lax.broadcasted_iota(jnp.int32, (CHUNK, CHUNK), 1)
    if seq_rows >= CHUNK:
        keep = r_i >= c_i
    else:
        keep = jnp.logical_and(r_i >= c_i, c_i >= (r_i // seq_rows) * seq_rows)
    for g in range(n_grp):
        cs = slice(g * CHUNK, (g + 1) * CHUNK)
        wm = jnp.where(keep, wmix_ref[g], 0.0).astype(BF16)
        u = jax.nn.gelu(proj[:, u0 + g * CHUNK:u0 + (g + 1) * CHUNK])
        vg = jax.nn.gelu(proj[:, g0 + g * CHUNK:g0 + (g + 1) * CHUNK])
        vn = vg * lax.rsqrt(jnp.mean(vg * vg, axis=-1, keepdims=True) + EPS) * gv_ref[:, cs]
        if gvo_ref is not None:
            gvo_ref[:, cs] = vn
        vnb = vn.astype(BF16)
        for c in range(tm // CHUNK):
            rs = slice(c * CHUNK, (c + 1) * CHUNK)
            mixed = jnp.dot(wm, vnb[rs], preferred_element_type=F32) + bmix_ref[:, cs]
            om_ref[rs, cs] = (u[rs] * mixed).astype(BF16)


def _normed_input(x_ref, sh_ref, sc_ref, g_ref):
    x = x_ref[...]
    ms = jnp.mean(x * x, axis=-1, keepdims=True)
    h = x * lax.rsqrt(ms + EPS) * g_ref[...]
    return (h * (1.0 + sc_ref[...]) + sh_ref[...]).astype(BF16)


def _inproj_prompt_kernel(x_ref, sh_ref, sc_ref, g1_ref, wnat_ref, wkvt_ref, gq_ref, gkc_ref,
                          gv_ref, qaux_ref, wmix_ref, bmix_ref,
                          kt_ref, vt_ref, qh_ref, kh_ref, vb_ref, om_ref,
                          *, tm, d_sb, d_mlp, seq_rows):
    n_heads = d_sb // HEAD_DIM
    hb = _normed_input(x_ref, sh_ref, sc_ref, g1_ref)
    proj = jnp.dot(hb, wnat_ref[...], preferred_element_type=F32)
    kvt = lax.dot_general(wkvt_ref[...], hb, NT_DIMS, preferred_element_type=F32)

    gkc = jnp.tile(gkc_ref[...], (1, tm // LANES))
    aux_rows = jnp.where(lax.broadcasted_iota(jnp.int32, (HEAD_DIM, tm), 0) == 0, 1.0, 0.0)
    for hd in range(n_heads):
        rs = slice(hd * HEAD_DIM, (hd + 1) * HEAD_DIM)
        blk = kvt[rs, :]
        kn = blk * lax.rsqrt(jnp.mean(blk * blk, axis=0, keepdims=True) + EPS) * gkc
        kt_ref[rs, :] = kn
        kh_ref[hd, 0:HEAD_DIM, :] = kn.astype(BF16)
        kh_ref[hd, HEAD_DIM:, :] = aux_rows.astype(BF16)
    vt_ref[...] = kvt[d_sb:, :]

    lo = lax.broadcasted_iota(jnp.int32, (tm, LANES), 1) < HEAD_DIM
    for p in range(d_sb // LANES):
        cs = slice(p * LANES, (p + 1) * LANES)
        qn = _two_head_norm(proj[:, p * LANES:(p + 1) * LANES], gq_ref[:, cs])
        vb_ref[:, cs] = proj[:, d_sb + p * LANES:d_sb + (p + 1) * LANES].astype(BF16)
        qh_ref[2 * p] = jnp.where(lo, qn, qaux_ref[2 * p:2 * p + 1, :]).astype(BF16)
        qr = pltpu.roll(qn, HEAD_DIM, axis=1)
        qh_ref[2 * p + 1] = jnp.where(lo, qr, qaux_ref[2 * p + 1:2 * p + 2, :]).astype(BF16)

    _spatial_gate(proj, 2 * d_sb, 2 * d_sb + d_mlp, gv_ref, wmix_ref, bmix_ref, om_ref, None,
                  tm, d_mlp // CHUNK, seq_rows)


def _inproj_sample_kernel(x_ref, sh_ref, sc_ref, g1_ref, win_ref, gq_ref, gk_ref, gv_ref,
                          wmix_ref, bmix_ref, k_ref, v_ref, qn_ref, gvo_ref, om_ref,
                          *, tm, d_sb, d_mlp, seq_rows):
    hb = _normed_input(x_ref, sh_ref, sc_ref, g1_ref)
    proj = jnp.dot(hb, win_ref[...], preferred_element_type=F32)
    for p in range(d_sb // LANES):
        cs = slice(p * LANES, (p + 1) * LANES)
        qn_ref[:, cs] = _two_head_norm(proj[:, p * LANES:(p + 1) * LANES], gq_ref[:, cs])
        k_ref[:, cs] = _two_head_norm(proj[:, d_sb + p * LANES:d_sb + (p + 1) * LANES],
                                      gk_ref[:, cs])
        v_ref[:, cs] = proj[:, 2 * d_sb + p * LANES:2 * d_sb + (p + 1) * LANES]
    _spatial_gate(proj, 3 * d_sb, 3 * d_sb + d_mlp, gv_ref, wmix_ref, bmix_ref, om_ref, gvo_ref,
                  tm, d_mlp // CHUNK, seq_rows)


def _inproj_prompt(x2, mod, n_batch, tm, wnat_b, wkvt_b, g1, gq, gkc, gv, qaux, wmix, bmix,
                   d_sb, d_mlp):
    rows, d = x2.shape
    n_heads = d_sb // HEAD_DIM
    t_len = rows // n_batch
    tpb = t_len // tm
    const = lambda i: (0, 0)
    in_specs = [
        pl.BlockSpec((tm, d), lambda i: (i, 0)),
        _mod_spec(False, 0, tm, d, tpb),
        _mod_spec(False, 1, tm, d, tpb),
        pl.BlockSpec((1, d), const),
        _resident(wnat_b.shape),
        _resident(wkvt_b.shape),
        pl.BlockSpec((1, d_sb), const),
        pl.BlockSpec((HEAD_DIM, LANES), const),
        pl.BlockSpec((1, d_mlp), const),
        pl.BlockSpec((n_heads, LANES), const),
        pl.BlockSpec(wmix.shape, lambda i: (0, 0, 0)),
        pl.BlockSpec((CHUNK, d_mlp), const),
    ]
    row_spec = lambda w: pl.BlockSpec((tm, w), lambda i: (i, 0))
    tmin_spec = pl.BlockSpec((None, d_sb, tm), lambda i: (i // tpb, 0, i % tpb))
    out_specs = [
        tmin_spec, tmin_spec,
        pl.BlockSpec((None, n_heads, tm, LANES), lambda i: (i // tpb, 0, i % tpb, 0)),
        pl.BlockSpec((None, n_heads, None, LANES, tm), lambda i: (i // tpb, 0, i % tpb, 0, 0)),
        row_spec(d_sb), row_spec(d_mlp)]
    out_shape = [
        jax.ShapeDtypeStruct((n_batch, d_sb, t_len), F32),
        jax.ShapeDtypeStruct((n_batch, d_sb, t_len), F32),
        jax.ShapeDtypeStruct((n_batch, n_heads, t_len, LANES), BF16),
        jax.ShapeDtypeStruct((n_batch, n_heads, tpb, LANES, tm), BF16),
        jax.ShapeDtypeStruct((rows, d_sb), BF16),
        jax.ShapeDtypeStruct((rows, d_mlp), BF16)]
    kern = functools.partial(_inproj_prompt_kernel, tm=tm, d_sb=d_sb, d_mlp=d_mlp, seq_rows=t_len)
    return pl.pallas_call(
        kern, grid=(rows // tm,), in_specs=in_specs, out_specs=out_specs, out_shape=out_shape,
        compiler_params=_cparams(("arbitrary",)), name="inproj_prompt",
    )(x2, mod, mod, g1, wnat_b, wkvt_b, gq, gkc, gv, qaux, wmix, bmix)


def _inproj_sample(x2, mod, seq_rows, win_b, g1, gq, gk, gv, wmix, bmix, d_sb, d_mlp):
    rows, d = x2.shape
    tm = rows
    const = lambda i: (0, 0)
    in_specs = [
        pl.BlockSpec((tm, d), lambda i: (i, 0)),
        _mod_spec(True, 0, tm, d, 1),
        _mod_spec(True, 1, tm, d, 1),
        pl.BlockSpec((1, d), const),
        _resident(win_b.shape),
        pl.BlockSpec((1, d_sb), const),
        pl.BlockSpec((1, d_sb), const),
        pl.BlockSpec((1, d_mlp), const),
        pl.BlockSpec(wmix.shape, lambda i: (0, 0, 0)),
        pl.BlockSpec((CHUNK, d_mlp), const),
    ]
    row_spec = lambda w: pl.BlockSpec((tm, w), lambda i: (i, 0))
    out_specs = [row_spec(d_sb), row_spec(d_sb), row_spec(d_sb), row_spec(d_mlp), row_spec(d_mlp)]
    out_shape = [jax.ShapeDtypeStruct((rows, d_sb), F32),
                 jax.ShapeDtypeStruct((rows, d_sb), F32),
                 jax.ShapeDtypeStruct((rows, d_sb), F32),
                 jax.ShapeDtypeStruct((rows, d_mlp), F32),
                 jax.ShapeDtypeStruct((rows, d_mlp), BF16)]
    kern = functools.partial(_inproj_sample_kernel, tm=tm, d_sb=d_sb, d_mlp=d_mlp,
                             seq_rows=seq_rows)
    return pl.pallas_call(
        kern, grid=(1,), in_specs=in_specs, out_specs=out_specs, out_shape=out_shape,
        compiler_params=_cparams(("arbitrary",)), name="inproj_sample",
    )(x2, mod, mod, g1, win_b, gq, gk, gv, wmix, bmix)


def _softplus2(z2):
    return jnp.maximum(z2, 0.0) + jnp.log(1.0 + jnp.exp2(-jnp.abs(z2))) * LOG2E


def _strict_upper(n):
    r = lax.broadcasted_iota(jnp.int32, (n, n), 0)
    c = lax.broadcasted_iota(jnp.int32, (n, n), 1)
    return jnp.where(r > c, 1.0, 0.0).astype(BF16)


def _sample_attention(pt_ref, q_ref, kn_ref, vn_ref, bias_ref, ck_hbm, cv_hbm, o_ref,
                      kbuf, vbuf, sem, qbd_ref, a_ref, acc_ref, acct_ref, car_ref,
                      *, n_seq, n_tok, n_heads, pages_per_chunk, n_chunks, n_buf, page, sub):
    total = n_seq * n_chunks
    d_sb = n_heads * HEAD_DIM
    rows = n_tok * n_heads
    ck = pages_per_chunk * page
    n_sub = ck // sub

    def chunk_copies(g, which):
        seq = g // n_chunks
        chunk = n_chunks - 1 - g % n_chunks
        slot = g % n_buf
        src, dst = ((ck_hbm, kbuf), (cv_hbm, vbuf))[which]
        cps = []
        for pg in range(pages_per_chunk):
            pid = pt_ref[seq, chunk * pages_per_chunk + pg]
            cols = pl.ds(pg * page, page)
            cps.append(pltpu.make_async_copy(src.at[pid], dst.at[slot, :, cols], sem.at[which, slot]))
        return cps

    def prime():
        for g0 in range(n_buf - 1):
            for cp in chunk_copies(g0, 0) + chunk_copies(g0, 1):
                cp.start()

    def own_mask():
        r_i = lax.broadcasted_iota(jnp.int32, (rows, d_sb), 0)
        c_i = lax.broadcasted_iota(jnp.int32, (rows, d_sb), 1)
        return (r_i % n_heads) == (c_i // HEAD_DIM)

    def begin_sequence(seq):
        q_nat = q_ref[seq]
        q_rep = jnp.concatenate(
            [jnp.broadcast_to(q_nat[t:t + 1, :], (n_heads, d_sb)) for t in range(n_tok)], axis=0)
        q_bd = jnp.where(own_mask(), q_rep, 0.0)
        qbd_ref[...] = q_bd.astype(BF16)
        bias = bias_ref[...]
        tok = lax.broadcasted_iota(jnp.int32, (rows, 1), 0) // n_heads
        acc = jnp.zeros((rows, d_sb), F32)
        car = jnp.zeros((rows, 1), F32)
        k_new = kn_ref[seq]
        v_new = vn_ref[seq]
        for s in range(n_tok - 1, -1, -1):
            z = jnp.sum(q_bd * k_new[s:s + 1, :], axis=-1, keepdims=True) + bias
            vis = tok > s
            sp = jnp.where(vis, _softplus2(z), 0.0)
            a = jnp.where(vis, jnp.exp2(z - sp - car), 0.0)
            acc = acc + a * v_new[s:s + 1, :]
            car = car + sp
        acc_ref[seq % 2] = acc
        acct_ref[seq % 2] = jnp.zeros(acct_ref.shape[1:], F32)
        car_ref[...] = jnp.broadcast_to(car, (rows, LANES))

    def end_sequence(seq):
        res = jnp.where(own_mask(), acc_ref[seq % 2] + acct_ref[seq % 2].T[0:rows, :], 0.0)
        o_ref[seq] = jnp.concatenate(
            [jnp.sum(res[t * n_heads:(t + 1) * n_heads], axis=0, keepdims=True)
             for t in range(n_tok)], axis=0)

    def k_side(g):
        slot = g % n_buf
        upper = _strict_upper(sub)
        z_all = jnp.dot(qbd_ref[...], kbuf[slot].astype(BF16),
                        preferred_element_type=F32) + bias_ref[...]
        sps = [_softplus2(z_all[:, s * sub:(s + 1) * sub]) for s in range(n_sub)]
        sums = [jnp.sum(sp, axis=-1, keepdims=True) for sp in sps]
        cr = car_ref[...]
        a_parts = [None] * n_sub
        later_all = jnp.dot(jnp.concatenate([sp.astype(BF16) for sp in sps], axis=0), upper,
                            preferred_element_type=F32)
        for s in range(n_sub - 1, -1, -1):
            later = later_all[s * rows:(s + 1) * rows]
            t = z_all[:, s * sub:(s + 1) * sub] - sps[s] - later - jnp.tile(cr, (1, sub // LANES))
            a_parts[s] = jnp.exp2(t).astype(BF16)
            cr = cr + sums[s]
        a_ref[g % 2] = jnp.concatenate(
            [jnp.concatenate(a_parts, axis=1), jnp.zeros((LANES - rows, ck), BF16)], axis=0)
        car_ref[...] = cr

    def v_side(g, a):
        par = (g // n_chunks) % 2
        acct_ref[par] += lax.dot_general(vbuf[g % n_buf].astype(BF16), a, NT_DIMS,
                                         preferred_element_type=F32)

    def step(g):
        has_k = g < total
        has_v = g >= 1

        @pl.when(has_k)
        def _():
            @pl.when(g % n_chunks == 0)
            def _():
                begin_sequence(g // n_chunks)

            for cp in chunk_copies(g, 0):
                cp.wait()

            @pl.when(g + n_buf - 1 < total)
            def _():
                for cp in chunk_copies(g + n_buf - 1, 0):
                    cp.start()

        @pl.when(has_v)
        def _():
            for cp in chunk_copies(g - 1, 1):
                cp.wait()

        @pl.when(jnp.logical_not(has_v))
        def _():
            k_side(g)

        @pl.when(jnp.logical_and(has_k, has_v))
        def _():
            v_side(g - 1, a_ref[(g - 1) % 2])
            k_side(g)

        @pl.when(jnp.logical_not(has_k))
        def _():
            v_side(g - 1, a_ref[(g - 1) % 2])

        @pl.when(g + n_buf - 1 < total)
        def _():
            for cp in chunk_copies(g + n_buf - 1, 1):
                cp.start()

        @pl.when(jnp.logical_and(has_v, (g - 1) % n_chunks == n_chunks - 1))
        def _():
            end_sequence((g - 1) // n_chunks)

    return prime, step


def _attn_kernel(pt_ref, q_ref, k_ref, v_ref, qs_ref, kn_ref, vn_ref, bias_ref, ck_hbm, cv_hbm,
                 o_ref, os_ref,
                 acc_ref, car_ref, kbuf, vbuf, sem, qbd_ref, sa_ref, sacc_ref, sacct_ref, scar_ref,
                 cnt_ref, *, blk, n_h, units_per_step, sample_cfg):
    b = pl.program_id(0)
    i = pl.program_id(1)
    prime, sample_step = _sample_attention(
        pt_ref, qs_ref, kn_ref, vn_ref, bias_ref, ck_hbm, cv_hbm, os_ref,
        kbuf, vbuf, sem, qbd_ref, sa_ref, sacc_ref, sacct_ref, scar_ref, **sample_cfg)
    n_sample_steps = sample_cfg["n_seq"] * sample_cfg["n_chunks"] + 1

    @pl.when(jnp.logical_and(b == 0, i == 0))
    def _():
        cnt_ref[0] = 0
        prime()

    def tick(k):
        assert k <= units_per_step
        u = cnt_ref[0]
        cnt_ref[0] = u + k
        g = u // units_per_step

        @pl.when(jnp.logical_and((u + k) // units_per_step > g, g < n_sample_steps))
        def _():
            sample_step(g)

    upper = _strict_upper(blk)
    reps = blk // LANES
    row = lax.broadcasted_iota(jnp.int32, (blk, blk), 0)
    col = lax.broadcasted_iota(jnp.int32, (blk, blk), 1)
    causal = col < row

    def tiles(js, mask):
        heads = range(n_h)
        tl = [(j, hh) for j in js for hh in heads]
        zs = [jnp.dot(q_ref[hh], k_ref[hh, j], preferred_element_type=F32) for j, hh in tl]
        sps = [_softplus2(z) for z in zs]
        if mask is not None:
            sps = [jnp.where(mask, sp, 0.0) for sp in sps]
        lbs = [z - sp for z, sp in zip(zs, sps)]
        later_all = jnp.dot(jnp.concatenate([sp.astype(BF16) for sp in sps], axis=0), upper,
                            preferred_element_type=F32)
        laters = [later_all[t * blk:(t + 1) * blk] for t in range(len(tl))]
        cars = []
        run = [car_ref[hh] for hh in heads]
        for t, (j, hh) in enumerate(tl):
            cars.append(run[hh])
            run[hh] = run[hh] + jnp.sum(sps[t], axis=-1, keepdims=True)
        for hh in heads:
            car_ref[hh] = run[hh]
        a_s = [jnp.exp2(lbs[t] - laters[t] - jnp.tile(cars[t], (1, reps))) for t in range(len(tl))]
        if mask is not None:
            a_s = [jnp.where(mask, a, 0.0) for a in a_s]
        for p in range(n_h // 2):
            tot = None
            for jn, j in enumerate(js):
                v = v_ref[pl.ds(pl.multiple_of(j * blk, blk), blk), p * LANES:(p + 1) * LANES]
                t0 = jn * n_h + 2 * p
                a_pair = jnp.concatenate([a_s[t0].astype(BF16), a_s[t0 + 1].astype(BF16)], axis=0)
                av = jnp.dot(a_pair, v, preferred_element_type=F32)
                tot = av if tot is None else tot + av
            acc_ref[2 * p] += tot[:blk]
            acc_ref[2 * p + 1] += tot[blk:]

    acc_ref[...] = jnp.zeros_like(acc_ref)
    car_ref[...] = jnp.zeros_like(car_ref)
    tiles([i], causal)
    tick(1)

    def body(n, c):
        tiles([i - 1 - n], None)
        tick(1)
        return c

    lax.fori_loop(0, i, body, 0)

    lo = lax.broadcasted_iota(jnp.int32, (blk, LANES), 1) < HEAD_DIM
    for p in range(n_h // 2):
        o_ref[:, p * LANES:(p + 1) * LANES] = jnp.where(
            lo, acc_ref[2 * p], acc_ref[2 * p + 1]).astype(o_ref.dtype)


def _attn(qh, kh, vb, page_table, q_nat, k_new, v_new, bias_rows, cache_kt, cache_vt):
    n_batch, n_heads, t_len, _ = qh.shape
    blk = kh.shape[-1]
    n_blk = t_len // blk
    d_sb = n_heads * HEAD_DIM
    n_seq, n_tok, _ = q_nat.shape
    n_pages = page_table.shape[1]
    page = cache_kt.shape[2]
    pages_per_chunk = SAMPLE_CHUNK_PAGES
    n_buf = SAMPLE_RING_SLOTS
    n_chunks = n_pages // pages_per_chunk
    assert n_chunks * pages_per_chunk == n_pages
    rows = n_tok * n_heads
    ck = pages_per_chunk * page
    units = n_batch * n_blk * (n_blk + 1) // 2
    units_per_step = units // (n_seq * n_chunks + 1)
    assert units_per_step >= 1, "not enough prompt work to pace the sample steps"
    sample_cfg = dict(n_seq=n_seq, n_tok=n_tok, n_heads=n_heads, pages_per_chunk=pages_per_chunk,
                      n_chunks=n_chunks, n_buf=n_buf, page=page, sub=blk)
    whole = lambda shape: pl.BlockSpec(shape, lambda b, i, pt: (0,) * len(shape))
    grid_spec = pltpu.PrefetchScalarGridSpec(
        num_scalar_prefetch=1,
        grid=(n_batch, n_blk),
        in_specs=[pl.BlockSpec((None, n_heads, blk, LANES), lambda b, i, pt: (b, 0, i, 0)),
                  pl.BlockSpec((None, n_heads, n_blk, LANES, blk), lambda b, i, pt: (b, 0, 0, 0, 0),
                               pipeline_mode=pl.Buffered(1)),
                  pl.BlockSpec((None, t_len, d_sb), lambda b, i, pt: (b, 0, 0),
                               pipeline_mode=pl.Buffered(1)),
                  whole((n_seq, n_tok, d_sb)), whole((n_seq, n_tok, d_sb)),
                  whole((n_seq, n_tok, d_sb)), whole((rows, 1)),
                  pl.BlockSpec(memory_space=pl.ANY),
                  pl.BlockSpec(memory_space=pl.ANY)],
        out_specs=[pl.BlockSpec((None, blk, d_sb), lambda b, i, pt: (b, i, 0)),
                   whole((n_seq, n_tok, d_sb))],
        scratch_shapes=[pltpu.VMEM((n_heads, blk, LANES), F32),
                        pltpu.VMEM((n_heads, blk, LANES), F32),
                        pltpu.VMEM((n_buf, d_sb, ck), F32),
                        pltpu.VMEM((n_buf, d_sb, ck), F32),
                        pltpu.SemaphoreType.DMA((2, n_buf)),
                        pltpu.VMEM((rows, d_sb), BF16),
                        pltpu.VMEM((2, LANES, ck), BF16),
                        pltpu.VMEM((2, rows, d_sb), F32),
                        pltpu.VMEM((2, d_sb, LANES), F32),
                        pltpu.VMEM((rows, LANES), F32),
                        pltpu.SMEM((1,), jnp.int32)],
    )
    kern = functools.partial(_attn_kernel, blk=blk, n_h=n_heads, units_per_step=units_per_step,
                             sample_cfg=sample_cfg)
    return pl.pallas_call(
        kern, grid_spec=grid_spec,
        out_shape=[jax.ShapeDtypeStruct((n_batch, t_len, d_sb), BF16),
                   jax.ShapeDtypeStruct((n_seq, n_tok, d_sb), F32)],
        compiler_params=_cparams(("arbitrary", "arbitrary")),
        name="attn",
    )(page_table, qh, kh, vb, q_nat, k_new, v_new, bias_rows, cache_kt, cache_vt)


def _outffn_kernel(*refs, tm, d_ff, fc, seq_rows, tiles_per_batch, has_state):
    if has_state:
        (x_ref, osb_ref, oml_ref, g1_ref, sh_ref, sc_ref, g2_ref, n2_ref, wout_ref, wup_ref,
         cw_ref, cb_ref, wdn_ref, p1_ref, p2_ref, y_ref, up_ref) = refs
    else:
        (x_ref, osb_ref, oml_ref, g1_ref, sh_ref, sc_ref, g2_ref, n2_ref, wout_ref, wup_ref,
         cw_ref, cb_ref, wdn_ref, y_ref, st_ref, carry_ref) = refs
    d_sb = osb_ref.shape[1]
    mix = (jnp.dot(osb_ref[...].astype(BF16), wout_ref[0:d_sb, :], preferred_element_type=F32)
           + jnp.dot(oml_ref[...], wout_ref[d_sb:, :], preferred_element_type=F32))
    x1 = x_ref[...] + g1_ref[...] * mix
    ms = jnp.mean(x1 * x1, axis=-1, keepdims=True)
    h2 = x1 * lax.rsqrt(ms + EPS) * n2_ref[...]
    h2 = (h2 * (1.0 + sc_ref[...]) + sh_ref[...]).astype(BF16)

    row = lax.broadcasted_iota(jnp.int32, (tm, 1), 0)
    if has_state:
        t_in = row % seq_rows
    else:
        t_in = row
        first = (pl.program_id(0) % tiles_per_batch) == 0

        @pl.when(first)
        def _():
            carry_ref[...] = jnp.zeros_like(carry_ref)

    def conv(u, cols):
        r1 = pltpu.roll(u, 1, axis=0)
        r2 = pltpu.roll(u, 2, axis=0)
        if has_state:
            up_ref[:, cols] = u
            prev1 = jnp.where(t_in >= 1, r1, p1_ref[:, cols])
            prev2 = jnp.where(t_in >= 2, r2, p2_ref[:, cols])
        else:
            c6 = carry_ref[CARRY_ROWS - 2:CARRY_ROWS - 1, cols]
            c7 = carry_ref[CARRY_ROWS - 1:CARRY_ROWS, cols]
            prev1 = jnp.where(t_in >= 1, r1, c7)
            prev2 = jnp.where(t_in >= 2, r2, jnp.where(t_in == 0, c6, c7))
            carry_ref[:, cols] = u[tm - CARRY_ROWS:, :]
        return (cb_ref[:, cols] + cw_ref[0:1, cols] * prev2 + cw_ref[1:2, cols] * prev1
                + cw_ref[2:3, cols] * u)

    acc = jnp.zeros((tm, x1.shape[1]), F32)
    for f in range(d_ff // fc):
        gc = slice(f * fc, (f + 1) * fc)
        vc = slice(d_ff + f * fc, d_ff + (f + 1) * fc)
        cg = conv(jnp.dot(h2, wup_ref[:, gc], preferred_element_type=F32), gc)
        cv = conv(jnp.dot(h2, wup_ref[:, vc], preferred_element_type=F32), vc)
        act = (cg * jax.nn.sigmoid(cg) * cv).astype(BF16)
        acc = acc + jnp.dot(act, wdn_ref[gc, :], preferred_element_type=F32)
    y_ref[...] = x1 + g2_ref[...] * acc

    if not has_state:
        @pl.when((pl.program_id(0) % tiles_per_batch) == tiles_per_batch - 1)
        def _():
            st_ref[...] = carry_ref[...]


def _outffn(x2, osb, oml, mod, per_row, n_batch, seq_rows, tm, n2, wout_b, wup_b, conv_w, conv_b,
            wdn_b, state_rows):
    rows, d = x2.shape
    d_ff = wdn_b.shape[0]
    d_sb = osb.shape[1]
    d_mlp = oml.shape[1]
    tiles = rows // tm
    tpb = max(tiles // n_batch, 1)
    has_state = state_rows is not None
    const = lambda i: (0, 0)
    row_spec = lambda w: pl.BlockSpec((tm, w), lambda i: (i, 0))
    in_specs = [
        row_spec(d), row_spec(d_sb), row_spec(d_mlp),
        _mod_spec(per_row, 2, tm, d, tpb),
        _mod_spec(per_row, 3, tm, d, tpb),
        _mod_spec(per_row, 4, tm, d, tpb),
        _mod_spec(per_row, 5, tm, d, tpb),
        pl.BlockSpec((1, d), const),
        _resident(wout_b.shape),
        _resident(wup_b.shape),
        pl.BlockSpec(conv_w.shape, const),
        pl.BlockSpec((1, 2 * d_ff), const),
        _resident(wdn_b.shape),
    ]
    args = [x2, osb, oml, mod, mod, mod, mod, n2, wout_b, wup_b, conv_w, conv_b, wdn_b]
    if has_state:
        in_specs += [_resident((rows, 2 * d_ff)), _resident((rows, 2 * d_ff))]
        args += list(state_rows)
        out_specs = [row_spec(d), row_spec(2 * d_ff)]
        out_shape = [jax.ShapeDtypeStruct((rows, d), F32),
                     jax.ShapeDtypeStruct((rows, 2 * d_ff), F32)]
        scratch = []
    else:
        out_specs = [row_spec(d),
                     pl.BlockSpec((None, CARRY_ROWS, 2 * d_ff), lambda i: (i // tpb, 0, 0))]
        out_shape = [jax.ShapeDtypeStruct((rows, d), F32),
                     jax.ShapeDtypeStruct((n_batch, CARRY_ROWS, 2 * d_ff), F32)]
        scratch = [pltpu.VMEM((CARRY_ROWS, 2 * d_ff), F32)]
    kern = functools.partial(_outffn_kernel, tm=tm, d_ff=d_ff, fc=d_ff // 2, seq_rows=seq_rows,
                             tiles_per_batch=tpb, has_state=has_state)
    return pl.pallas_call(
        kern, grid=(tiles,), in_specs=in_specs, out_specs=out_specs, out_shape=out_shape,
        scratch_shapes=scratch,
        compiler_params=_cparams(("arbitrary",)),
        name="outffn_sample" if has_state else "outffn_prompt",
    )(*args)


def _layer(l, yp, ys, c_prompt, c_sample, cache_k, cache_v, state_conv, page_table,
           norm1_g, norm2_g, w_ada, b_ada, w_in, q_norm_g, k_norm_g, sb_bias, v_norm_g,
           w_s, b_s, w_out, w_up, conv_w, conv_b, w_down):
    bp, tp, d = yp.shape
    db, ts, _ = ys.shape
    d_ff = w_down.shape[1]
    n_heads = sb_bias.shape[1]
    d_sb = n_heads * HEAD_DIM
    n_grp = v_norm_g.shape[1]
    d_mlp = n_grp * CHUNK
    n_pool, page = cache_k.shape[1], cache_k.shape[2]

    n_c = bp + db
    pad = (-n_c) % 8
    c_all = jnp.concatenate([c_prompt, c_sample, jnp.zeros((pad, d), F32)], axis=0)
    mod = _ada(c_all, w_ada[l], b_ada[l])
    mod_p = mod[:bp].reshape(bp, N_MOD, 1, d)
    mod_s = jnp.repeat(mod[bp:n_c].reshape(db, N_MOD, d).transpose(1, 0, 2), ts, axis=1)

    win_b = w_in[l].astype(BF16)
    wnat_b = jnp.concatenate([win_b[:, :d_sb], win_b[:, 2 * d_sb:]], axis=1)
    wkvt_b = win_b[:, d_sb:3 * d_sb].T
    wout_b = w_out[l].astype(BF16)
    wup_b = w_up[l].astype(BF16)
    wdn_b = w_down[l].astype(BF16)
    g1 = norm1_g[l].reshape(1, d)
    n2 = norm2_g[l].reshape(1, d)
    gq = (jnp.tile(q_norm_g[l], n_heads) * (LOG2E * HEAD_DIM ** -0.5)).reshape(1, d_sb)
    sbb2 = sb_bias[l] * LOG2E
    gk = jnp.tile(k_norm_g[l], n_heads).reshape(1, d_sb)
    gkc = jnp.broadcast_to(k_norm_g[l][:, None], (HEAD_DIM, LANES))
    gv = v_norm_g[l].reshape(1, d_mlp)
    qaux = jnp.where((jnp.arange(LANES) == HEAD_DIM)[None, :], sbb2[:, None], 0.0).astype(F32)
    cb = conv_b[l].reshape(1, 2 * d_ff)

    bmix_p = jnp.repeat(b_s[l].T, CHUNK, axis=1)
    xp2 = yp.reshape(bp * tp, d)
    kt, vt, qh, kh, vb, oml_p = _inproj_prompt(
        xp2, mod_p, bp, ATTN_BLK, wnat_b, wkvt_b, g1, gq, gkc, gv, qaux, w_s[l], bmix_p,
        d_sb, d_mlp)

    rows_s = db * ts
    wmix_s = jnp.tile(w_s[l][:, :ts, :ts], (1, CHUNK // ts, CHUNK // ts))
    bmix_s = jnp.repeat(jnp.tile(b_s[l][:, :ts], (1, CHUNK // ts)).T, CHUNK, axis=1)
    xs2 = ys.reshape(rows_s, d)
    ks_, vs_, qn_s, gv_s, oml_s = _inproj_sample(
        xs2, mod_s, ts, win_b, g1, gq, gk, gv, wmix_s, bmix_s, d_sb, d_mlp)
    bias_rows = jnp.tile(sbb2, ts).reshape(ts * n_heads, 1)
    cache_kt = cache_k[l].transpose(0, 2, 3, 1).reshape(n_pool, d_sb, page)
    cache_vt = cache_v[l].transpose(0, 2, 3, 1).reshape(n_pool, d_sb, page)
    osb_p, osb_s = _attn(qh, kh, vb.reshape(bp, tp, d_sb), page_table,
                         qn_s.reshape(db, ts, d_sb), ks_.reshape(db, ts, d_sb),
                         vs_.reshape(db, ts, d_sb), bias_rows, cache_kt, cache_vt)
    yp2, st_p = _outffn(xp2, osb_p.reshape(bp * tp, d_sb), oml_p, mod_p, False, bp, tp, 512, n2,
                        wout_b, wup_b, conv_w[l], cb, wdn_b, None)
    st = state_conv[l]
    p1 = jnp.concatenate([st[:, 1:2], jnp.zeros((db, ts - 1, 2 * d_ff), F32)], axis=1)
    p2 = jnp.concatenate([st, jnp.zeros((db, ts - 2, 2 * d_ff), F32)], axis=1)
    ys2, up_s = _outffn(xs2, osb_s.reshape(rows_s, d_sb), oml_s, mod_s, True, db, ts, rows_s, n2,
                        wout_b, wup_b, conv_w[l], cb, wdn_b,
                        (p1.reshape(rows_s, 2 * d_ff), p2.reshape(rows_s, 2 * d_ff)))

    outs = (kt.reshape(bp, n_heads, HEAD_DIM, tp).transpose(0, 3, 1, 2),
            vt.reshape(bp, n_heads, HEAD_DIM, tp).transpose(0, 3, 1, 2),
            ks_.reshape(db, ts, n_heads, HEAD_DIM), vs_.reshape(db, ts, n_heads, HEAD_DIM),
            gv_s.reshape(db, ts, n_grp, CHUNK),
            st_p[:, CARRY_ROWS - (CONV_W - 1):],
            up_s.reshape(db, ts, 2 * d_ff)[:, ts - (CONV_W - 1):])
    return yp2.reshape(bp, tp, d), ys2.reshape(db, ts, d), outs


def kernel(x_prompt, x_sample, c_prompt, c_sample, cache_k, cache_v, state_conv, page_table,
           norm1_g, norm2_g, w_ada, b_ada, w_in, q_norm_g, k_norm_g, sb_bias, v_norm_g,
           w_s, b_s, w_out, w_up, conv_w, conv_b, w_down):
    depth = w_in.shape[0]
    yp, ys = x_prompt, x_sample
    per_layer = []
    for l in range(depth):
        yp, ys, outs = _layer(l, yp, ys, c_prompt, c_sample, cache_k, cache_v, state_conv,
                              page_table, norm1_g, norm2_g, w_ada, b_ada, w_in, q_norm_g,
                              k_norm_g, sb_bias, v_norm_g, w_s, b_s, w_out, w_up, conv_w,
                              conv_b, w_down)
        per_layer.append(outs)
    stacked = tuple(jnp.stack([o[i] for o in per_layer]) for i in range(7))
    return (yp, ys) + stacked
```

```python
import functools

import jax
import jax.numpy as jnp
from jax import lax
from jax.experimental import pallas as pl
from jax.experimental.pallas import tpu as pltpu

F32 = jnp.float32
BF16 = jnp.bfloat16

EPS = 1e-6
LOG2E = 1.4426950408889634
HEAD_DIM = 64
LANES = 128
CHUNK = 128
N_MOD = 6
CONV_W = 3
CARRY_ROWS = 8
ATTN_BLK = 256
PROMPT_ROWS = 512
SAMPLE_CHUNK_PAGES = 16
SAMPLE_RING_SLOTS = 3

VMEM_LIMIT = 56 * 1024 * 1024

NT_DIMS = (((1,), (1,)), ((), ()))


def _cparams(sem):
    return pltpu.CompilerParams(dimension_semantics=sem, vmem_limit_bytes=VMEM_LIMIT)


def _resident(shape):
    nd = len(shape)
    return pl.BlockSpec(shape, lambda *_: (0,) * nd, pipeline_mode=pl.Buffered(1))


def _mod_spec(per_row, k, tm, d, tiles_per_batch):
    if per_row:
        return pl.BlockSpec((None, tm, d), lambda i: (k, i, 0))
    return pl.BlockSpec((None, None, 1, d), lambda i: (i // tiles_per_batch, k, 0, 0))


def _ada_kernel(c_ref, w_ref, b_ref, o_ref):
    c = c_ref[...]
    s = c * jax.nn.sigmoid(c)
    o_ref[...] = jnp.dot(s, w_ref[...], preferred_element_type=F32,
                         precision=lax.Precision.HIGHEST) + b_ref[...]


def _ada(c_all, w_ada, b_ada):
    rows, d = c_all.shape
    n = w_ada.shape[1]
    bn = n // 4
    return pl.pallas_call(
        _ada_kernel,
        grid=(n // bn,),
        in_specs=[pl.BlockSpec((rows, d), lambda j: (0, 0)),
                  pl.BlockSpec((d, bn), lambda j: (0, j)),
                  pl.BlockSpec((1, bn), lambda j: (0, j))],
        out_specs=pl.BlockSpec((rows, bn), lambda j: (0, j)),
        out_shape=jax.ShapeDtypeStruct((rows, n), F32),
        compiler_params=_cparams(("arbitrary",)),
        name="ada",
    )(c_all, w_ada, b_ada.reshape(1, n))


def _two_head_norm(blk, g):
    lo = lax.broadcasted_iota(jnp.int32, blk.shape, 1) < HEAD_DIM
    sq = blk * blk
    s_lo = jnp.sum(jnp.where(lo, sq, 0.0), axis=-1, keepdims=True)
    s_hi = jnp.sum(jnp.where(lo, 0.0, sq), axis=-1, keepdims=True)
    r = lax.rsqrt(jnp.where(lo, s_lo, s_hi) * (1.0 / HEAD_DIM) + EPS)
    return blk * r * g


def _spatial_gate(proj, u0, g0, gv_ref, wmix_ref, bmix_ref, om_ref, gvo_ref, tm, n_grp, seq_rows):
    r_i = lax.broadcasted_iota(jnp.int32, (CHUNK, CHUNK), 0)
    c_i = lax.broadcasted_iota(jnp.int32, (CHUNK, CHUNK), 1)
    if seq_rows >= CHUNK:
        keep = r_i >= c_i
    else:
        keep = jnp.logical_and(r_i >= c_i, c_i >= (r_i // seq_rows) * seq_rows)
    for g in range(n_grp):
        cs = slice(g * CHUNK, (g + 1) * CHUNK)
        wm = jnp.where(keep, wmix_ref[g], 0.0).astype(BF16)
        u = jax.nn.gelu(proj[:, u0 + g * CHUNK:u0 + (g + 1) * CHUNK])
        vg = jax.nn.gelu(proj[:, g0 + g * CHUNK:g0 + (g + 1) * CHUNK])
        vn = vg * lax.rsqrt(jnp.mean(vg * vg, axis=-1, keepdims=True) + EPS) * gv_ref[:, cs]
        if gvo_ref is not None:
            gvo_ref[:, cs] = vn
        vnb = vn.astype(BF16)
        for c in range(tm // CHUNK):
            rs = slice(c * CHUNK, (c + 1) * CHUNK)
            mixed = jnp.dot(wm, vnb[rs], preferred_element_type=F32) + bmix_ref[:, cs]
            om_ref[rs, cs] = (u[rs] * mixed).astype(BF16)


def _normed_input(x_ref, sh_ref, sc_ref, g_ref):
    x = x_ref[...]
    ms = jnp.mean(x * x, axis=-1, keepdims=True)
    h = x * lax.rsqrt(ms + EPS) * g_ref[...]
    return (h * (1.0 + sc_ref[...]) + sh_ref[...]).astype(BF16)


def _inproj_prompt_kernel(x_ref, sh_ref, sc_ref, g1_ref, wnat_ref, wkvt_ref, gq_ref, gkc_ref,
                          gv_ref, qaux_ref, wmix_ref, bmix_ref,
                          kt_ref, vt_ref, qh_ref, kh_ref, vb_ref, om_ref,
                          *, tm, d_sb, d_mlp, seq_rows):
    n_heads = d_sb // HEAD_DIM
    hb = _normed_input(x_ref, sh_ref, sc_ref, g1_ref)
    proj = jnp.dot(hb, wnat_ref[...], preferred_element_type=F32)
    kvt = lax.dot_general(wkvt_ref[...], hb, NT_DIMS, preferred_element_type=F32)

    gkc = jnp.tile(gkc_ref[...], (1, tm // LANES))
    kb = kh_ref.shape[-1]
    aux_rows = jnp.where(lax.broadcasted_iota(jnp.int32, (HEAD_DIM, kb), 0) == 0, 1.0, 0.0)
    for hd in range(n_heads):
        rs = slice(hd * HEAD_DIM, (hd + 1) * HEAD_DIM)
        blk = kvt[rs, :]
        kn = blk * lax.rsqrt(jnp.mean(blk * blk, axis=0, keepdims=True) + EPS) * gkc
        kt_ref[rs, :] = kn
        for c in range(tm // kb):
            kh_ref[hd, c, 0:HEAD_DIM, :] = kn[:, c * kb:(c + 1) * kb].astype(BF16)
            kh_ref[hd, c, HEAD_DIM:, :] = aux_rows.astype(BF16)
    vt_ref[...] = kvt[d_sb:, :]

    lo = lax.broadcasted_iota(jnp.int32, (tm, LANES), 1) < HEAD_DIM
    for p in range(d_sb // LANES):
        cs = slice(p * LANES, (p + 1) * LANES)
        qn = _two_head_norm(proj[:, p * LANES:(p + 1) * LANES], gq_ref[:, cs])
        vb_ref[:, cs] = proj[:, d_sb + p * LANES:d_sb + (p + 1) * LANES].astype(BF16)
        qh_ref[2 * p] = jnp.where(lo, qn, qaux_ref[2 * p:2 * p + 1, :]).astype(BF16)
        qr = pltpu.roll(qn, HEAD_DIM, axis=1)
        qh_ref[2 * p + 1] = jnp.where(lo, qr, qaux_ref[2 * p + 1:2 * p + 2, :]).astype(BF16)

    _spatial_gate(proj, 2 * d_sb, 2 * d_sb + d_mlp, gv_ref, wmix_ref, bmix_ref, om_ref, None,
                  tm, d_mlp // CHUNK, seq_rows)


def _inproj_sample_kernel(x_ref, sh_ref, sc_ref, g1_ref, win_ref, gq_ref, gk_ref, gv_ref,
                          wmix_ref, bmix_ref, k_ref, v_ref, qn_ref, gvo_ref, om_ref,
                          *, tm, d_sb, d_mlp, seq_rows):
    hb = _normed_input(x_ref, sh_ref, sc_ref, g1_ref)
    proj = jnp.dot(hb, win_ref[...], preferred_element_type=F32)
    for p in range(d_sb // LANES):
        cs = slice(p * LANES, (p + 1) * LANES)
        qn_ref[:, cs] = _two_head_norm(proj[:, p * LANES:(p + 1) * LANES], gq_ref[:, cs])
        k_ref[:, cs] = _two_head_norm(proj[:, d_sb + p * LANES:d_sb + (p + 1) * LANES],
                                      gk_ref[:, cs])
        v_ref[:, cs] = proj[:, 2 * d_sb + p * LANES:2 * d_sb + (p + 1) * LANES]
    _spatial_gate(proj, 3 * d_sb, 3 * d_sb + d_mlp, gv_ref, wmix_ref, bmix_ref, om_ref, gvo_ref,
                  tm, d_mlp // CHUNK, seq_rows)


def _inproj_prompt(x2, mod, n_batch, tm, wnat_b, wkvt_b, g1, gq, gkc, gv, qaux, wmix, bmix,
                   d_sb, d_mlp):
    rows, d = x2.shape
    n_heads = d_sb // HEAD_DIM
    t_len = rows // n_batch
    tpb = t_len // tm
    const = lambda i: (0, 0)
    in_specs = [
        pl.BlockSpec((tm, d), lambda i: (i, 0)),
        _mod_spec(False, 0, tm, d, tpb),
        _mod_spec(False, 1, tm, d, tpb),
        pl.BlockSpec((1, d), const),
        _resident(wnat_b.shape),
        _resident(wkvt_b.shape),
        pl.BlockSpec((1, d_sb), const),
        pl.BlockSpec((HEAD_DIM, LANES), const),
        pl.BlockSpec((1, d_mlp), const),
        pl.BlockSpec((n_heads, LANES), const),
        pl.BlockSpec(wmix.shape, lambda i: (0, 0, 0)),
        pl.BlockSpec((CHUNK, d_mlp), const),
    ]
    row_spec = lambda w: pl.BlockSpec((tm, w), lambda i: (i, 0))
    tmin_spec = pl.BlockSpec((None, d_sb, tm), lambda i: (i // tpb, 0, i % tpb))
    out_specs = [
        tmin_spec, tmin_spec,
        pl.BlockSpec((None, n_heads, tm, LANES), lambda i: (i // tpb, 0, i % tpb, 0)),
        pl.BlockSpec((None, n_heads, tm // ATTN_BLK, LANES, ATTN_BLK),
                     lambda i: (i // tpb, 0, i % tpb, 0, 0)),
        row_spec(d_sb), row_spec(d_mlp)]
    out_shape = [
        jax.ShapeDtypeStruct((n_batch, d_sb, t_len), F32),
        jax.ShapeDtypeStruct((n_batch, d_sb, t_len), F32),
        jax.ShapeDtypeStruct((n_batch, n_heads, t_len, LANES), BF16),
        jax.ShapeDtypeStruct((n_batch, n_heads, t_len // ATTN_BLK, LANES, ATTN_BLK), BF16),
        jax.ShapeDtypeStruct((rows, d_sb), BF16),
        jax.ShapeDtypeStruct((rows, d_mlp), BF16)]
    kern = functools.partial(_inproj_prompt_kernel, tm=tm, d_sb=d_sb, d_mlp=d_mlp, seq_rows=t_len)
    return pl.pallas_call(
        kern, grid=(rows // tm,), in_specs=in_specs, out_specs=out_specs, out_shape=out_shape,
        compiler_params=_cparams(("arbitrary",)), name="inproj_prompt",
    )(x2, mod, mod, g1, wnat_b, wkvt_b, gq, gkc, gv, qaux, wmix, bmix)


def _inproj_sample(x2, mod, seq_rows, win_b, g1, gq, gk, gv, wmix, bmix, d_sb, d_mlp):
    rows, d = x2.shape
    tm = rows
    const = lambda i: (0, 0)
    in_specs = [
        pl.BlockSpec((tm, d), lambda i: (i, 0)),
        _mod_spec(True, 0, tm, d, 1),
        _mod_spec(True, 1, tm, d, 1),
        pl.BlockSpec((1, d), const),
        _resident(win_b.shape),
        pl.BlockSpec((1, d_sb), const),
        pl.BlockSpec((1, d_sb), const),
        pl.BlockSpec((1, d_mlp), const),
        pl.BlockSpec(wmix.shape, lambda i: (0, 0, 0)),
        pl.BlockSpec((CHUNK, d_mlp), const),
    ]
    row_spec = lambda w: pl.BlockSpec((tm, w), lambda i: (i, 0))
    out_specs = [row_spec(d_sb), row_spec(d_sb), row_spec(d_sb), row_spec(d_mlp), row_spec(d_mlp)]
    out_shape = [jax.ShapeDtypeStruct((rows, d_sb), F32),
                 jax.ShapeDtypeStruct((rows, d_sb), F32),
                 jax.ShapeDtypeStruct((rows, d_sb), F32),
                 jax.ShapeDtypeStruct((rows, d_mlp), F32),
                 jax.ShapeDtypeStruct((rows, d_mlp), BF16)]
    kern = functools.partial(_inproj_sample_kernel, tm=tm, d_sb=d_sb, d_mlp=d_mlp,
                             seq_rows=seq_rows)
    return pl.pallas_call(
        kern, grid=(1,), in_specs=in_specs, out_specs=out_specs, out_shape=out_shape,
        compiler_params=_cparams(("arbitrary",)), name="inproj_sample",
    )(x2, mod, mod, g1, win_b, gq, gk, gv, wmix, bmix)


def _softplus2(z2):
    return jnp.maximum(z2, 0.0) + jnp.log(1.0 + jnp.exp2(-jnp.abs(z2))) * LOG2E


def _strict_upper(n):
    r = lax.broadcasted_iota(jnp.int32, (n, n), 0)
    c = lax.broadcasted_iota(jnp.int32, (n, n), 1)
    return jnp.where(r > c, 1.0, 0.0).astype(BF16)


def _sample_attention(pt_ref, q_ref, kn_ref, vn_ref, bias_ref, ck_hbm, cv_hbm, o_ref,
                      kbuf, vbuf, sem, qbd_ref, a_ref, acc_ref, acct_ref, car_ref,
                      *, n_seq, n_tok, n_heads, pages_per_chunk, n_chunks, n_buf, page, sub):
    total = n_seq * n_chunks
    d_sb = n_heads * HEAD_DIM
    rows = n_tok * n_heads
    ck = pages_per_chunk * page
    n_sub = ck // sub

    def chunk_copies(g, which):
        seq = g // n_chunks
        chunk = n_chunks - 1 - g % n_chunks
        slot = g % n_buf
        src, dst = ((ck_hbm, kbuf), (cv_hbm, vbuf))[which]
        cps = []
        for pg in range(pages_per_chunk):
            pid = pt_ref[seq, chunk * pages_per_chunk + pg]
            cols = pl.ds(pg * page, page)
            cps.append(pltpu.make_async_copy(src.at[pid], dst.at[slot, :, cols], sem.at[which, slot]))
        return cps

    def prime():
        for g0 in range(n_buf - 1):
            for cp in chunk_copies(g0, 0) + chunk_copies(g0, 1):
                cp.start()

    def own_mask():
        r_i = lax.broadcasted_iota(jnp.int32, (rows, d_sb), 0)
        c_i = lax.broadcasted_iota(jnp.int32, (rows, d_sb), 1)
        return (r_i % n_heads) == (c_i // HEAD_DIM)

    def begin_sequence(seq):
        q_nat = q_ref[seq]
        q_rep = jnp.concatenate(
            [jnp.broadcast_to(q_nat[t:t + 1, :], (n_heads, d_sb)) for t in range(n_tok)], axis=0)
        q_bd = jnp.where(own_mask(), q_rep, 0.0)
        qbd_ref[...] = q_bd.astype(BF16)
        bias = bias_ref[...]
        tok = lax.broadcasted_iota(jnp.int32, (rows, 1), 0) // n_heads
        acc = jnp.zeros((rows, d_sb), F32)
        car = jnp.zeros((rows, 1), F32)
        k_new = kn_ref[seq]
        v_new = vn_ref[seq]
        for s in range(n_tok - 1, -1, -1):
            z = jnp.sum(q_bd * k_new[s:s + 1, :], axis=-1, keepdims=True) + bias
            vis = tok > s
            sp = jnp.where(vis, _softplus2(z), 0.0)
            a = jnp.where(vis, jnp.exp2(z - sp - car), 0.0)
            acc = acc + a * v_new[s:s + 1, :]
            car = car + sp
        acc_ref[seq % 2] = acc
        acct_ref[seq % 2] = jnp.zeros(acct_ref.shape[1:], F32)
        car_ref[...] = jnp.broadcast_to(car, (rows, LANES))

    def end_sequence(seq):
        res = jnp.where(own_mask(), acc_ref[seq % 2] + acct_ref[seq % 2].T[0:rows, :], 0.0)
        o_ref[seq] = jnp.concatenate(
            [jnp.sum(res[t * n_heads:(t + 1) * n_heads], axis=0, keepdims=True)
             for t in range(n_tok)], axis=0)

    def k_side(g):
        slot = g % n_buf
        upper = _strict_upper(sub)
        z_all = jnp.dot(qbd_ref[...], kbuf[slot].astype(BF16),
                        preferred_element_type=F32) + bias_ref[...]
        sps = [_softplus2(z_all[:, s * sub:(s + 1) * sub]) for s in range(n_sub)]
        sums = [jnp.sum(sp, axis=-1, keepdims=True) for sp in sps]
        cr = car_ref[...]
        a_parts = [None] * n_sub
        later_all = jnp.dot(jnp.concatenate([sp.astype(BF16) for sp in sps], axis=0), upper,
                            preferred_element_type=F32)
        for s in range(n_sub - 1, -1, -1):
            later = later_all[s * rows:(s + 1) * rows]
            t = z_all[:, s * sub:(s + 1) * sub] - sps[s] - later - jnp.tile(cr, (1, sub // LANES))
            a_parts[s] = jnp.exp2(t).astype(BF16)
            cr = cr + sums[s]
        a_ref[g % 2] = jnp.concatenate(
            [jnp.concatenate(a_parts, axis=1), jnp.zeros((LANES - rows, ck), BF16)], axis=0)
        car_ref[...] = cr

    def v_side(g, a):
        par = (g // n_chunks) % 2
        acct_ref[par] += lax.dot_general(vbuf[g % n_buf].astype(BF16), a, NT_DIMS,
                                         preferred_element_type=F32)

    def step(g):
        has_k = g < total
        has_v = g >= 1

        @pl.when(has_k)
        def _():
            @pl.when(g % n_chunks == 0)
            def _():
                begin_sequence(g // n_chunks)

            for cp in chunk_copies(g, 0):
                cp.wait()

            @pl.when(g + n_buf - 1 < total)
            def _():
                for cp in chunk_copies(g + n_buf - 1, 0):
                    cp.start()

        @pl.when(has_v)
        def _():
            for cp in chunk_copies(g - 1, 1):
                cp.wait()

        @pl.when(jnp.logical_not(has_v))
        def _():
            k_side(g)

        @pl.when(jnp.logical_and(has_k, has_v))
        def _():
            v_side(g - 1, a_ref[(g - 1) % 2])
            k_side(g)

        @pl.when(jnp.logical_not(has_k))
        def _():
            v_side(g - 1, a_ref[(g - 1) % 2])

        @pl.when(g + n_buf - 1 < total)
        def _():
            for cp in chunk_copies(g + n_buf - 1, 1):
                cp.start()

        @pl.when(jnp.logical_and(has_v, (g - 1) % n_chunks == n_chunks - 1))
        def _():
            end_sequence((g - 1) // n_chunks)

    return prime, step


def _attn_kernel(pt_ref, q_ref, k_ref, v_ref, qs_ref, kn_ref, vn_ref, bias_ref, ck_hbm, cv_hbm,
                 o_ref, os_ref,
                 acc_ref, car_ref, z_ref, kbuf, vbuf, sem, qbd_ref, sa_ref, sacc_ref, sacct_ref,
                 scar_ref, cnt_ref, *, blk, n_h, units_per_step, sample_cfg):
    b = pl.program_id(0)
    i = pl.program_id(1)
    prime, sample_step = _sample_attention(
        pt_ref, qs_ref, kn_ref, vn_ref, bias_ref, ck_hbm, cv_hbm, os_ref,
        kbuf, vbuf, sem, qbd_ref, sa_ref, sacc_ref, sacct_ref, scar_ref, **sample_cfg)
    n_sample_steps = sample_cfg["n_seq"] * sample_cfg["n_chunks"] + 1

    @pl.when(jnp.logical_and(b == 0, i == 0))
    def _():
        cnt_ref[0] = 0
        prime()

    def tick(k):
        assert k <= units_per_step
        u = cnt_ref[0]
        cnt_ref[0] = u + k
        g = u // units_per_step

        @pl.when(jnp.logical_and((u + k) // units_per_step > g, g < n_sample_steps))
        def _():
            sample_step(g)

    upper = _strict_upper(blk)
    reps = blk // LANES
    row = lax.broadcasted_iota(jnp.int32, (blk, blk), 0)
    col = lax.broadcasted_iota(jnp.int32, (blk, blk), 1)
    causal = col < row

    def logits(j):
        return [jnp.dot(q_ref[hh], k_ref[hh, j], preferred_element_type=F32) for hh in range(n_h)]

    def tiles(zs, j, mask, z_next_ref):
        heads = range(n_h)
        nxt = logits(jnp.maximum(j - 1, 0))
        for hh in heads:
            z_next_ref[hh] = nxt[hh]
        sps = [_softplus2(z) for z in zs]
        if mask is not None:
            sps = [jnp.where(mask, sp, 0.0) for sp in sps]
        lbs = [z - sp for z, sp in zip(zs, sps)]
        later_all = jnp.dot(jnp.concatenate([sp.astype(BF16) for sp in sps], axis=0), upper,
                            preferred_element_type=F32)
        laters = [later_all[hh * blk:(hh + 1) * blk] for hh in heads]
        cars = [car_ref[hh] for hh in heads]
        for hh in heads:
            car_ref[hh] = cars[hh] + jnp.sum(sps[hh], axis=-1, keepdims=True)
        a_s = [jnp.exp2(lbs[hh] - laters[hh] - jnp.tile(cars[hh], (1, reps))) for hh in heads]
        if mask is not None:
            a_s = [jnp.where(mask, a, 0.0) for a in a_s]
        for p in range(n_h // 2):
            v = v_ref[pl.ds(pl.multiple_of(j * blk, blk), blk), p * LANES:(p + 1) * LANES]
            a_pair = jnp.concatenate([a_s[2 * p].astype(BF16), a_s[2 * p + 1].astype(BF16)], axis=0)
            av = jnp.dot(a_pair, v, preferred_element_type=F32)
            acc_ref[2 * p] += av[:blk]
            acc_ref[2 * p + 1] += av[blk:]

    acc_ref[...] = jnp.zeros_like(acc_ref)
    car_ref[...] = jnp.zeros_like(car_ref)
    tiles(logits(i), i, causal, z_ref)
    tick(1)

    def body(n, c):
        tiles([z_ref[hh] for hh in range(n_h)], i - 1 - n, None, z_ref)
        tick(1)
        return c

    lax.fori_loop(0, i, body, 0)

    lo = lax.broadcasted_iota(jnp.int32, (blk, LANES), 1) < HEAD_DIM
    for p in range(n_h // 2):
        o_ref[:, p * LANES:(p + 1) * LANES] = jnp.where(
            lo, acc_ref[2 * p], acc_ref[2 * p + 1]).astype(o_ref.dtype)


def _attn(qh, kh, vb, page_table, q_nat, k_new, v_new, bias_rows, cache_kt, cache_vt):
    n_batch, n_heads, t_len, _ = qh.shape
    blk = kh.shape[-1]
    n_blk = t_len // blk
    d_sb = n_heads * HEAD_DIM
    n_seq, n_tok, _ = q_nat.shape
    n_pages = page_table.shape[1]
    page = cache_kt.shape[2]
    pages_per_chunk = SAMPLE_CHUNK_PAGES
    n_buf = SAMPLE_RING_SLOTS
    n_chunks = n_pages // pages_per_chunk
    assert n_chunks * pages_per_chunk == n_pages
    rows = n_tok * n_heads
    ck = pages_per_chunk * page
    units = n_batch * n_blk * (n_blk + 1) // 2
    units_per_step = units // (n_seq * n_chunks + 1)
    assert units_per_step >= 1, "not enough prompt work to pace the sample steps"
    sample_cfg = dict(n_seq=n_seq, n_tok=n_tok, n_heads=n_heads, pages_per_chunk=pages_per_chunk,
                      n_chunks=n_chunks, n_buf=n_buf, page=page, sub=blk)
    whole = lambda shape: pl.BlockSpec(shape, lambda b, i, pt: (0,) * len(shape))
    grid_spec = pltpu.PrefetchScalarGridSpec(
        num_scalar_prefetch=1,
        grid=(n_batch, n_blk),
        in_specs=[pl.BlockSpec((None, n_heads, blk, LANES), lambda b, i, pt: (b, 0, i, 0)),
                  pl.BlockSpec((None, n_heads, n_blk, LANES, blk), lambda b, i, pt: (b, 0, 0, 0, 0),
                               pipeline_mode=pl.Buffered(1)),
                  pl.BlockSpec((None, t_len, d_sb), lambda b, i, pt: (b, 0, 0),
                               pipeline_mode=pl.Buffered(1)),
                  whole((n_seq, n_tok, d_sb)), whole((n_seq, n_tok, d_sb)),
                  whole((n_seq, n_tok, d_sb)), whole((rows, 1)),
                  pl.BlockSpec(memory_space=pl.ANY),
                  pl.BlockSpec(memory_space=pl.ANY)],
        out_specs=[pl.BlockSpec((None, blk, d_sb), lambda b, i, pt: (b, i, 0)),
                   whole((n_seq, n_tok, d_sb))],
        scratch_shapes=[pltpu.VMEM((n_heads, blk, LANES), F32),
                        pltpu.VMEM((n_heads, blk, LANES), F32),
                        pltpu.VMEM((n_heads, blk, blk), F32),
                        pltpu.VMEM((n_buf, d_sb, ck), F32),
                        pltpu.VMEM((n_buf, d_sb, ck), F32),
                        pltpu.SemaphoreType.DMA((2, n_buf)),
                        pltpu.VMEM((rows, d_sb), BF16),
                        pltpu.VMEM((2, LANES, ck), BF16),
                        pltpu.VMEM((2, rows, d_sb), F32),
                        pltpu.VMEM((2, d_sb, LANES), F32),
                        pltpu.VMEM((rows, LANES), F32),
                        pltpu.SMEM((1,), jnp.int32)],
    )
    kern = functools.partial(_attn_kernel, blk=blk, n_h=n_heads, units_per_step=units_per_step,
                             sample_cfg=sample_cfg)
    return pl.pallas_call(
        kern, grid_spec=grid_spec,
        out_shape=[jax.ShapeDtypeStruct((n_batch, t_len, d_sb), BF16),
                   jax.ShapeDtypeStruct((n_seq, n_tok, d_sb), F32)],
        compiler_params=_cparams(("arbitrary", "arbitrary")),
        name="attn",
    )(page_table, qh, kh, vb, q_nat, k_new, v_new, bias_rows, cache_kt, cache_vt)


def _outffn_kernel(*refs, tm, d_ff, fc, seq_rows, tiles_per_batch, has_state):
    if has_state:
        (x_ref, osb_ref, oml_ref, g1_ref, sh_ref, sc_ref, g2_ref, n2_ref, wout_ref, wup_ref,
         cw_ref, cb_ref, wdn_ref, p1_ref, p2_ref, y_ref, up_ref) = refs
    else:
        (x_ref, osb_ref, oml_ref, g1_ref, sh_ref, sc_ref, g2_ref, n2_ref, wout_ref, wup_ref,
         cw_ref, cb_ref, wdn_ref, y_ref, st_ref, carry_ref) = refs
    d_sb = osb_ref.shape[1]
    mix = (jnp.dot(osb_ref[...].astype(BF16), wout_ref[0:d_sb, :], preferred_element_type=F32)
           + jnp.dot(oml_ref[...], wout_ref[d_sb:, :], preferred_element_type=F32))
    x1 = x_ref[...] + g1_ref[...] * mix
    ms = jnp.mean(x1 * x1, axis=-1, keepdims=True)
    h2 = x1 * lax.rsqrt(ms + EPS) * n2_ref[...]
    h2 = (h2 * (1.0 + sc_ref[...]) + sh_ref[...]).astype(BF16)

    row = lax.broadcasted_iota(jnp.int32, (tm, 1), 0)
    if has_state:
        t_in = row % seq_rows
    else:
        t_in = row
        first = (pl.program_id(0) % tiles_per_batch) == 0

        @pl.when(first)
        def _():
            carry_ref[...] = jnp.zeros_like(carry_ref)

    def conv(u, cols):
        r1 = pltpu.roll(u, 1, axis=0)
        r2 = pltpu.roll(u, 2, axis=0)
        if has_state:
            up_ref[:, cols] = u
            prev1 = jnp.where(t_in >= 1, r1, p1_ref[:, cols])
            prev2 = jnp.where(t_in >= 2, r2, p2_ref[:, cols])
        else:
            c6 = carry_ref[CARRY_ROWS - 2:CARRY_ROWS - 1, cols]
            c7 = carry_ref[CARRY_ROWS - 1:CARRY_ROWS, cols]
            prev1 = jnp.where(t_in >= 1, r1, c7)
            prev2 = jnp.where(t_in >= 2, r2, jnp.where(t_in == 0, c6, c7))
            carry_ref[:, cols] = u[tm - CARRY_ROWS:, :]
        return (cb_ref[:, cols] + cw_ref[0:1, cols] * prev2 + cw_ref[1:2, cols] * prev1
                + cw_ref[2:3, cols] * u)

    acc = jnp.zeros((tm, x1.shape[1]), F32)
    for f in range(d_ff // fc):
        gc = slice(f * fc, (f + 1) * fc)
        vc = slice(d_ff + f * fc, d_ff + (f + 1) * fc)
        cg = conv(jnp.dot(h2, wup_ref[:, gc], preferred_element_type=F32), gc)
        cv = conv(jnp.dot(h2, wup_ref[:, vc], preferred_element_type=F32), vc)
        act = (cg * jax.nn.sigmoid(cg) * cv).astype(BF16)
        acc = acc + jnp.dot(act, wdn_ref[gc, :], preferred_element_type=F32)
    y_ref[...] = x1 + g2_ref[...] * acc

    if not has_state:
        @pl.when((pl.program_id(0) % tiles_per_batch) == tiles_per_batch - 1)
        def _():
            st_ref[...] = carry_ref[...]


def _outffn(x2, osb, oml, mod, per_row, n_batch, seq_rows, tm, n2, wout_b, wup_b, conv_w, conv_b,
            wdn_b, state_rows):
    rows, d = x2.shape
    d_ff = wdn_b.shape[0]
    d_sb = osb.shape[1]
    d_mlp = oml.shape[1]
    tiles = rows // tm
    tpb = max(tiles // n_batch, 1)
    has_state = state_rows is not None
    const = lambda i: (0, 0)
    row_spec = lambda w: pl.BlockSpec((tm, w), lambda i: (i, 0))
    in_specs = [
        row_spec(d), row_spec(d_sb), row_spec(d_mlp),
        _mod_spec(per_row, 2, tm, d, tpb),
        _mod_spec(per_row, 3, tm, d, tpb),
        _mod_spec(per_row, 4, tm, d, tpb),
        _mod_spec(per_row, 5, tm, d, tpb),
        pl.BlockSpec((1, d), const),
        _resident(wout_b.shape),
        _resident(wup_b.shape),
        pl.BlockSpec(conv_w.shape, const),
        pl.BlockSpec((1, 2 * d_ff), const),
        _resident(wdn_b.shape),
    ]
    args = [x2, osb, oml, mod, mod, mod, mod, n2, wout_b, wup_b, conv_w, conv_b, wdn_b]
    if has_state:
        in_specs += [_resident((rows, 2 * d_ff)), _resident((rows, 2 * d_ff))]
        args += list(state_rows)
        out_specs = [row_spec(d), row_spec(2 * d_ff)]
        out_shape = [jax.ShapeDtypeStruct((rows, d), F32),
                     jax.ShapeDtypeStruct((rows, 2 * d_ff), F32)]
        scratch = []
    else:
        out_specs = [row_spec(d),
                     pl.BlockSpec((None, CARRY_ROWS, 2 * d_ff), lambda i: (i // tpb, 0, 0))]
        out_shape = [jax.ShapeDtypeStruct((rows, d), F32),
                     jax.ShapeDtypeStruct((n_batch, CARRY_ROWS, 2 * d_ff), F32)]
        scratch = [pltpu.VMEM((CARRY_ROWS, 2 * d_ff), F32)]
    kern = functools.partial(_outffn_kernel, tm=tm, d_ff=d_ff, fc=d_ff // 2, seq_rows=seq_rows,
                             tiles_per_batch=tpb, has_state=has_state)
    return pl.pallas_call(
        kern, grid=(tiles,), in_specs=in_specs, out_specs=out_specs, out_shape=out_shape,
        scratch_shapes=scratch,
        compiler_params=_cparams(("arbitrary",)),
        name="outffn_sample" if has_state else "outffn_prompt",
    )(*args)


def _layer(l, yp, ys, c_prompt, c_sample, cache_k, cache_v, state_conv, page_table,
           norm1_g, norm2_g, w_ada, b_ada, w_in, q_norm_g, k_norm_g, sb_bias, v_norm_g,
           w_s, b_s, w_out, w_up, conv_w, conv_b, w_down):
    bp, tp, d = yp.shape
    db, ts, _ = ys.shape
    d_ff = w_down.shape[1]
    n_heads = sb_bias.shape[1]
    d_sb = n_heads * HEAD_DIM
    n_grp = v_norm_g.shape[1]
    d_mlp = n_grp * CHUNK
    n_pool, page = cache_k.shape[1], cache_k.shape[2]

    n_c = bp + db
    pad = (-n_c) % 8
    c_all = jnp.concatenate([c_prompt, c_sample, jnp.zeros((pad, d), F32)], axis=0)
    mod = _ada(c_all, w_ada[l], b_ada[l])
    mod_p = mod[:bp].reshape(bp, N_MOD, 1, d)
    mod_s = jnp.repeat(mod[bp:n_c].reshape(db, N_MOD, d).transpose(1, 0, 2), ts, axis=1)

    win_b = w_in[l].astype(BF16)
    wnat_b = jnp.concatenate([win_b[:, :d_sb], win_b[:, 2 * d_sb:]], axis=1)
    wkvt_b = win_b[:, d_sb:3 * d_sb].T
    wout_b = w_out[l].astype(BF16)
    wup_b = w_up[l].astype(BF16)
    wdn_b = w_down[l].astype(BF16)
    g1 = norm1_g[l].reshape(1, d)
    n2 = norm2_g[l].reshape(1, d)
    gq = (jnp.tile(q_norm_g[l], n_heads) * (LOG2E * HEAD_DIM ** -0.5)).reshape(1, d_sb)
    sbb2 = sb_bias[l] * LOG2E
    gk = jnp.tile(k_norm_g[l], n_heads).reshape(1, d_sb)
    gkc = jnp.broadcast_to(k_norm_g[l][:, None], (HEAD_DIM, LANES))
    gv = v_norm_g[l].reshape(1, d_mlp)
    qaux = jnp.where((jnp.arange(LANES) == HEAD_DIM)[None, :], sbb2[:, None], 0.0).astype(F32)
    cb = conv_b[l].reshape(1, 2 * d_ff)

    bmix_p = jnp.repeat(b_s[l].T, CHUNK, axis=1)
    xp2 = yp.reshape(bp * tp, d)
    kt, vt, qh, kh, vb, oml_p = _inproj_prompt(
        xp2, mod_p, bp, PROMPT_ROWS, wnat_b, wkvt_b, g1, gq, gkc, gv, qaux, w_s[l], bmix_p,
        d_sb, d_mlp)

    rows_s = db * ts
    wmix_s = jnp.tile(w_s[l][:, :ts, :ts], (1, CHUNK // ts, CHUNK // ts))
    bmix_s = jnp.repeat(jnp.tile(b_s[l][:, :ts], (1, CHUNK // ts)).T, CHUNK, axis=1)
    xs2 = ys.reshape(rows_s, d)
    ks_, vs_, qn_s, gv_s, oml_s = _inproj_sample(
        xs2, mod_s, ts, win_b, g1, gq, gk, gv, wmix_s, bmix_s, d_sb, d_mlp)
    bias_rows = jnp.tile(sbb2, ts).reshape(ts * n_heads, 1)
    cache_kt = cache_k[l].transpose(0, 2, 3, 1).reshape(n_pool, d_sb, page)
    cache_vt = cache_v[l].transpose(0, 2, 3, 1).reshape(n_pool, d_sb, page)
    osb_p, osb_s = _attn(qh, kh, vb.reshape(bp, tp, d_sb), page_table,
                         qn_s.reshape(db, ts, d_sb), ks_.reshape(db, ts, d_sb),
                         vs_.reshape(db, ts, d_sb), bias_rows, cache_kt, cache_vt)
    yp2, st_p = _outffn(xp2, osb_p.reshape(bp * tp, d_sb), oml_p, mod_p, False, bp, tp,
                        PROMPT_ROWS, n2,
                        wout_b, wup_b, conv_w[l], cb, wdn_b, None)
    st = state_conv[l]
    p1 = jnp.concatenate([st[:, 1:2], jnp.zeros((db, ts - 1, 2 * d_ff), F32)], axis=1)
    p2 = jnp.concatenate([st, jnp.zeros((db, ts - 2, 2 * d_ff), F32)], axis=1)
    ys2, up_s = _outffn(xs2, osb_s.reshape(rows_s, d_sb), oml_s, mod_s, True, db, ts, rows_s, n2,
                        wout_b, wup_b, conv_w[l], cb, wdn_b,
                        (p1.reshape(rows_s, 2 * d_ff), p2.reshape(rows_s, 2 * d_ff)))

    outs = (kt.reshape(bp, n_heads, HEAD_DIM, tp).transpose(0, 3, 1, 2),
            vt.reshape(bp, n_heads, HEAD_DIM, tp).transpose(0, 3, 1, 2),
            ks_.reshape(db, ts, n_heads, HEAD_DIM), vs_.reshape(db, ts, n_heads, HEAD_DIM),
            gv_s.reshape(db, ts, n_grp, CHUNK),
            st_p[:, CARRY_ROWS - (CONV_W - 1):],
            up_s.reshape(db, ts, 2 * d_ff)[:, ts - (CONV_W - 1):])
    return yp2.reshape(bp, tp, d), ys2.reshape(db, ts, d), outs


def kernel(x_prompt, x_sample, c_prompt, c_sample, cache_k, cache_v, state_conv, page_table,
           norm1_g, norm2_g, w_ada, b_ada, w_in, q_norm_g, k_norm_g, sb_bias, v_norm_g,
           w_s, b_s, w_out, w_up, conv_w, conv_b, w_down):
    depth = w_in.shape[0]
    yp, ys = x_prompt, x_sample
    per_layer = []
    for l in range(depth):
        yp, ys, outs = _layer(l, yp, ys, c_prompt, c_sample, cache_k, cache_v, state_conv,
                              page_table, norm1_g, norm2_g, w_ada, b_ada, w_in, q_norm_g,
                              k_norm_g, sb_bias, v_norm_g, w_s, b_s, w_out, w_up, conv_w,
                              conv_b, w_down)
        per_layer.append(outs)
    stacked = tuple(jnp.stack([o[i] for o in per_layer]) for i in range(7))
    return (yp, ys) + stacked
```

```python
import functools

import jax
import jax.numpy as jnp
from jax import lax
from jax.experimental import pallas as pl
from jax.experimental.pallas import tpu as pltpu

F32 = jnp.float32
BF16 = jnp.bfloat16

EPS = 1e-6
LOG2E = 1.4426950408889634
HEAD_DIM = 64
LANES = 128
CHUNK = 128
N_MOD = 6
CONV_W = 3
CARRY_ROWS = 8
ATTN_BLK = 256
PROMPT_ROWS = 512
SAMPLE_CHUNK_PAGES = 16
SAMPLE_RING_SLOTS = 3

VMEM_LIMIT = 56 * 1024 * 1024

NT_DIMS = (((1,), (1,)), ((), ()))


def _cparams(sem):
    return pltpu.CompilerParams(dimension_semantics=sem, vmem_limit_bytes=VMEM_LIMIT)


def _resident(shape):
    nd = len(shape)
    return pl.BlockSpec(shape, lambda *_: (0,) * nd, pipeline_mode=pl.Buffered(1))


def _mod_spec(per_row, k, tm, d, tiles_per_batch):
    if per_row:
        return pl.BlockSpec((None, tm, d), lambda i: (k, i, 0))
    return pl.BlockSpec((None, None, 1, d), lambda i: (i // tiles_per_batch, k, 0, 0))


def _ada_kernel(c_ref, w_ref, b_ref, o_ref):
    c = c_ref[...]
    s = c * jax.nn.sigmoid(c)
    o_ref[...] = jnp.dot(s, w_ref[...], preferred_element_type=F32,
                         precision=lax.Precision.HIGHEST) + b_ref[...]


def _ada(c_all, w_ada, b_ada):
    rows, d = c_all.shape
    n = w_ada.shape[1]
    bn = n // 4
    return pl.pallas_call(
        _ada_kernel,
        grid=(n // bn,),
        in_specs=[pl.BlockSpec((rows, d), lambda j: (0, 0)),
                  pl.BlockSpec((d, bn), lambda j: (0, j)),
                  pl.BlockSpec((1, bn), lambda j: (0, j))],
        out_specs=pl.BlockSpec((rows, bn), lambda j: (0, j)),
        out_shape=jax.ShapeDtypeStruct((rows, n), F32),
        compiler_params=_cparams(("arbitrary",)),
        name="ada",
    )(c_all, w_ada, b_ada.reshape(1, n))


def _two_head_norm(blk, g):
    lo = lax.broadcasted_iota(jnp.int32, blk.shape, 1) < HEAD_DIM
    sq = blk * blk
    s_lo = jnp.sum(jnp.where(lo, sq, 0.0), axis=-1, keepdims=True)
    s_hi = jnp.sum(jnp.where(lo, 0.0, sq), axis=-1, keepdims=True)
    r = lax.rsqrt(jnp.where(lo, s_lo, s_hi) * (1.0 / HEAD_DIM) + EPS)
    return blk * r * g


def _spatial_gate(proj, u0, g0, gv_ref, wmix_ref, bmix_ref, om_ref, gvo_ref, tm, n_grp, seq_rows):
    r_i = lax.broadcasted_iota(jnp.int32, (CHUNK, CHUNK), 0)
    c_i = lax.broadcasted_iota(jnp.int32, (CHUNK, CHUNK), 1)
    if seq_rows >= CHUNK:
        keep = r_i >= c_i
        mixing = lambda w: w
    else:
        keep = jnp.logical_and(r_i >= c_i, c_i >= (r_i // seq_rows) * seq_rows)
        pick = jnp.where(c_i == r_i % seq_rows, 1.0, 0.0)

        def mixing(w):
            rows_picked = jnp.dot(pick, w, preferred_element_type=F32,
                                  precision=lax.Precision.HIGHEST)
            return lax.dot_general(rows_picked, pick, NT_DIMS, preferred_element_type=F32,
                                   precision=lax.Precision.HIGHEST)
    for g in range(n_grp):
        cs = slice(g * CHUNK, (g + 1) * CHUNK)
        wm = jnp.where(keep, mixing(wmix_ref[g]), 0.0).astype(BF16)
        u = jax.nn.gelu(proj[:, u0 + g * CHUNK:u0 + (g + 1) * CHUNK])
        vg = jax.nn.gelu(proj[:, g0 + g * CHUNK:g0 + (g + 1) * CHUNK])
        vn = vg * lax.rsqrt(jnp.mean(vg * vg, axis=-1, keepdims=True) + EPS) * gv_ref[:, cs]
        if gvo_ref is not None:
            gvo_ref[:, cs] = vn
        vnb = vn.astype(BF16)
        for c in range(tm // CHUNK):
            rs = slice(c * CHUNK, (c + 1) * CHUNK)
            mixed = jnp.dot(wm, vnb[rs], preferred_element_type=F32) + bmix_ref[:, cs]
            om_ref[rs, cs] = (u[rs] * mixed).astype(BF16)


def _normed_input(x_ref, sh_ref, sc_ref, g_ref):
    x = x_ref[...]
    ms = jnp.mean(x * x, axis=-1, keepdims=True)
    h = x * lax.rsqrt(ms + EPS) * g_ref[...]
    return (h * (1.0 + sc_ref[...]) + sh_ref[...]).astype(BF16)


def _inproj_prompt_kernel(x_ref, sh_ref, sc_ref, g1_ref, wnat_ref, wkvt_ref, gq_ref, gkc_ref,
                          gv_ref, qaux_ref, wmix_ref, bmix_ref,
                          kt_ref, vt_ref, qh_ref, kh_ref, vb_ref, om_ref,
                          *, tm, d_sb, d_mlp, seq_rows):
    n_heads = d_sb // HEAD_DIM
    hb = _normed_input(x_ref, sh_ref, sc_ref, g1_ref)
    proj = jnp.dot(hb, wnat_ref[...], preferred_element_type=F32)
    kvt = lax.dot_general(wkvt_ref[...], hb, NT_DIMS, preferred_element_type=F32)

    gkc = jnp.tile(gkc_ref[...], (1, tm // LANES))
    kb = kh_ref.shape[-1]
    aux_rows = jnp.where(lax.broadcasted_iota(jnp.int32, (HEAD_DIM, kb), 0) == 0, 1.0, 0.0)
    for hd in range(n_heads):
        rs = slice(hd * HEAD_DIM, (hd + 1) * HEAD_DIM)
        blk = kvt[rs, :]
        kn = blk * lax.rsqrt(jnp.mean(blk * blk, axis=0, keepdims=True) + EPS) * gkc
        kt_ref[rs, :] = kn
        for c in range(tm // kb):
            kh_ref[hd, c, 0:HEAD_DIM, :] = kn[:, c * kb:(c + 1) * kb].astype(BF16)
            kh_ref[hd, c, HEAD_DIM:, :] = aux_rows.astype(BF16)
    vt_ref[...] = kvt[d_sb:, :]

    lo = lax.broadcasted_iota(jnp.int32, (tm, LANES), 1) < HEAD_DIM
    for p in range(d_sb // LANES):
        cs = slice(p * LANES, (p + 1) * LANES)
        qn = _two_head_norm(proj[:, p * LANES:(p + 1) * LANES], gq_ref[:, cs])
        vb_ref[:, cs] = proj[:, d_sb + p * LANES:d_sb + (p + 1) * LANES].astype(BF16)
        qh_ref[2 * p] = jnp.where(lo, qn, qaux_ref[2 * p:2 * p + 1, :]).astype(BF16)
        qr = pltpu.roll(qn, HEAD_DIM, axis=1)
        qh_ref[2 * p + 1] = jnp.where(lo, qr, qaux_ref[2 * p + 1:2 * p + 2, :]).astype(BF16)

    _spatial_gate(proj, 2 * d_sb, 2 * d_sb + d_mlp, gv_ref, wmix_ref, bmix_ref, om_ref, None,
                  tm, d_mlp // CHUNK, seq_rows)


def _inproj_sample_kernel(x_ref, sh_ref, sc_ref, g1_ref, win_ref, gq_ref, gk_ref, gv_ref,
                          wmix_ref, bmix_ref, k_ref, v_ref, qn_ref, gvo_ref, om_ref,
                          *, tm, d_sb, d_mlp, seq_rows):
    hb = _normed_input(x_ref, sh_ref, sc_ref, g1_ref)
    proj = jnp.dot(hb, win_ref[...], preferred_element_type=F32)
    for p in range(d_sb // LANES):
        cs = slice(p * LANES, (p + 1) * LANES)
        qn_ref[:, cs] = _two_head_norm(proj[:, p * LANES:(p + 1) * LANES], gq_ref[:, cs])
        k_ref[:, cs] = _two_head_norm(proj[:, d_sb + p * LANES:d_sb + (p + 1) * LANES],
                                      gk_ref[:, cs])
        v_ref[:, cs] = proj[:, 2 * d_sb + p * LANES:2 * d_sb + (p + 1) * LANES]
    _spatial_gate(proj, 3 * d_sb, 3 * d_sb + d_mlp, gv_ref, wmix_ref, bmix_ref, om_ref, gvo_ref,
                  tm, d_mlp // CHUNK, seq_rows)


def _inproj_prompt(x2, mod, n_batch, tm, wnat_b, wkvt_b, g1, gq, gkc, gv, qaux, wmix, bmix,
                   d_sb, d_mlp):
    rows, d = x2.shape
    n_heads = d_sb // HEAD_DIM
    t_len = rows // n_batch
    tpb = t_len // tm
    const = lambda i: (0, 0)
    in_specs = [
        pl.BlockSpec((tm, d), lambda i: (i, 0)),
        _mod_spec(False, 0, tm, d, tpb),
        _mod_spec(False, 1, tm, d, tpb),
        pl.BlockSpec((1, d), const),
        _resident(wnat_b.shape),
        _resident(wkvt_b.shape),
        pl.BlockSpec((1, d_sb), const),
        pl.BlockSpec((HEAD_DIM, LANES), const),
        pl.BlockSpec((1, d_mlp), const),
        pl.BlockSpec((n_heads, LANES), const),
        pl.BlockSpec(wmix.shape, lambda i: (0, 0, 0)),
        pl.BlockSpec((CHUNK, d_mlp), const),
    ]
    row_spec = lambda w: pl.BlockSpec((tm, w), lambda i: (i, 0))
    tmin_spec = pl.BlockSpec((None, d_sb, tm), lambda i: (i // tpb, 0, i % tpb))
    out_specs = [
        tmin_spec, tmin_spec,
        pl.BlockSpec((None, n_heads, tm, LANES), lambda i: (i // tpb, 0, i % tpb, 0)),
        pl.BlockSpec((None, n_heads, tm // ATTN_BLK, LANES, ATTN_BLK),
                     lambda i: (i // tpb, 0, i % tpb, 0, 0)),
        row_spec(d_sb), row_spec(d_mlp)]
    out_shape = [
        jax.ShapeDtypeStruct((n_batch, d_sb, t_len), F32),
        jax.ShapeDtypeStruct((n_batch, d_sb, t_len), F32),
        jax.ShapeDtypeStruct((n_batch, n_heads, t_len, LANES), BF16),
        jax.ShapeDtypeStruct((n_batch, n_heads, t_len // ATTN_BLK, LANES, ATTN_BLK), BF16),
        jax.ShapeDtypeStruct((rows, d_sb), BF16),
        jax.ShapeDtypeStruct((rows, d_mlp), BF16)]
    kern = functools.partial(_inproj_prompt_kernel, tm=tm, d_sb=d_sb, d_mlp=d_mlp, seq_rows=t_len)
    return pl.pallas_call(
        kern, grid=(rows // tm,), in_specs=in_specs, out_specs=out_specs, out_shape=out_shape,
        compiler_params=_cparams(("arbitrary",)), name="inproj_prompt",
    )(x2, mod, mod, g1, wnat_b, wkvt_b, gq, gkc, gv, qaux, wmix, bmix)


def _inproj_sample(x2, mod, seq_rows, win_b, g1, gq, gk, gv, wmix, bmix, d_sb, d_mlp):
    rows, d = x2.shape
    tm = rows
    const = lambda i: (0, 0)
    in_specs = [
        pl.BlockSpec((tm, d), lambda i: (i, 0)),
        _mod_spec(True, 0, tm, d, 1),
        _mod_spec(True, 1, tm, d, 1),
        pl.BlockSpec((1, d), const),
        _resident(win_b.shape),
        pl.BlockSpec((1, d_sb), const),
        pl.BlockSpec((1, d_sb), const),
        pl.BlockSpec((1, d_mlp), const),
        pl.BlockSpec(wmix.shape, lambda i: (0, 0, 0)),
        pl.BlockSpec((CHUNK, d_mlp), const),
    ]
    row_spec = lambda w: pl.BlockSpec((tm, w), lambda i: (i, 0))
    out_specs = [row_spec(d_sb), row_spec(d_sb), row_spec(d_sb), row_spec(d_mlp), row_spec(d_mlp)]
    out_shape = [jax.ShapeDtypeStruct((rows, d_sb), F32),
                 jax.ShapeDtypeStruct((rows, d_sb), F32),
                 jax.ShapeDtypeStruct((rows, d_sb), F32),
                 jax.ShapeDtypeStruct((rows, d_mlp), F32),
                 jax.ShapeDtypeStruct((rows, d_mlp), BF16)]
    kern = functools.partial(_inproj_sample_kernel, tm=tm, d_sb=d_sb, d_mlp=d_mlp,
                             seq_rows=seq_rows)
    return pl.pallas_call(
        kern, grid=(1,), in_specs=in_specs, out_specs=out_specs, out_shape=out_shape,
        compiler_params=_cparams(("arbitrary",)), name="inproj_sample",
    )(x2, mod, mod, g1, win_b, gq, gk, gv, wmix, bmix)


def _softplus2(z2):
    return jnp.maximum(z2, 0.0) + jnp.log(1.0 + jnp.exp2(-jnp.abs(z2))) * LOG2E


def _strict_upper(n):
    r = lax.broadcasted_iota(jnp.int32, (n, n), 0)
    c = lax.broadcasted_iota(jnp.int32, (n, n), 1)
    return jnp.where(r > c, 1.0, 0.0).astype(BF16)


def _sample_attention(pt_ref, q_ref, kn_ref, vn_ref, bias_ref, ck_hbm, cv_hbm, o_ref,
                      kbuf, vbuf, sem, qbd_ref, a_ref, acc_ref, acct_ref, car_ref,
                      *, n_seq, n_tok, n_heads, pages_per_chunk, n_chunks, n_buf, page, sub):
    total = n_seq * n_chunks
    d_sb = n_heads * HEAD_DIM
    rows = n_tok * n_heads
    ck = pages_per_chunk * page
    n_sub = ck // sub

    def chunk_copies(g, which):
        seq = g // n_chunks
        chunk = n_chunks - 1 - g % n_chunks
        slot = g % n_buf
        src, dst = ((ck_hbm, kbuf), (cv_hbm, vbuf))[which]
        cps = []
        for pg in range(pages_per_chunk):
            pid = pt_ref[seq, chunk * pages_per_chunk + pg]
            cols = pl.ds(pg * page, page)
            cps.append(pltpu.make_async_copy(src.at[pid], dst.at[slot, :, cols], sem.at[which, slot]))
        return cps

    def prime():
        for g0 in range(n_buf - 1):
            for cp in chunk_copies(g0, 0) + chunk_copies(g0, 1):
                cp.start()

    def own_mask():
        r_i = lax.broadcasted_iota(jnp.int32, (rows, d_sb), 0)
        c_i = lax.broadcasted_iota(jnp.int32, (rows, d_sb), 1)
        return (r_i % n_heads) == (c_i // HEAD_DIM)

    def begin_sequence(seq):
        q_nat = q_ref[seq]
        q_rep = jnp.concatenate(
            [jnp.broadcast_to(q_nat[t:t + 1, :], (n_heads, d_sb)) for t in range(n_tok)], axis=0)
        q_bd = jnp.where(own_mask(), q_rep, 0.0)
        qbd_ref[...] = q_bd.astype(BF16)
        bias = bias_ref[...]
        tok = lax.broadcasted_iota(jnp.int32, (rows, 1), 0) // n_heads
        acc = jnp.zeros((rows, d_sb), F32)
        car = jnp.zeros((rows, 1), F32)
        k_new = kn_ref[seq]
        v_new = vn_ref[seq]
        for s in range(n_tok - 1, -1, -1):
            z = jnp.sum(q_bd * k_new[s:s + 1, :], axis=-1, keepdims=True) + bias
            vis = tok > s
            sp = jnp.where(vis, _softplus2(z), 0.0)
            a = jnp.where(vis, jnp.exp2(z - sp - car), 0.0)
            acc = acc + a * v_new[s:s + 1, :]
            car = car + sp
        acc_ref[seq % 2] = acc
        acct_ref[seq % 2] = jnp.zeros(acct_ref.shape[1:], F32)
        car_ref[...] = jnp.broadcast_to(car, (rows, LANES))

    def end_sequence(seq):
        res = jnp.where(own_mask(), acc_ref[seq % 2] + acct_ref[seq % 2].T[0:rows, :], 0.0)
        o_ref[seq] = jnp.concatenate(
            [jnp.sum(res[t * n_heads:(t + 1) * n_heads], axis=0, keepdims=True)
             for t in range(n_tok)], axis=0)

    def k_logits(g, s):
        cols = pl.ds(s * sub, sub)
        return jnp.dot(qbd_ref[...], kbuf[g % n_buf, :, cols].astype(BF16),
                       preferred_element_type=F32) + bias_ref[...]

    def v_partial(g, s):
        cols = pl.ds(s * sub, sub)
        return lax.dot_general(vbuf[g % n_buf, :, cols].astype(BF16), a_ref[g % 2, :, cols],
                               NT_DIMS, preferred_element_type=F32)

    def k_weights(g, zs):
        upper = _strict_upper(sub)
        sps = [_softplus2(z) for z in zs]
        sums = [jnp.sum(sp, axis=-1, keepdims=True) for sp in sps]
        cr = car_ref[...]
        a_parts = [None] * n_sub
        later_all = jnp.dot(jnp.concatenate([sp.astype(BF16) for sp in sps], axis=0), upper,
                            preferred_element_type=F32)
        for s in range(n_sub - 1, -1, -1):
            later = later_all[s * rows:(s + 1) * rows]
            t = zs[s] - sps[s] - later - jnp.tile(cr, (1, sub // LANES))
            a_parts[s] = jnp.exp2(t).astype(BF16)
            cr = cr + sums[s]
        a_ref[g % 2] = jnp.concatenate(
            [jnp.concatenate(a_parts, axis=1), jnp.zeros((LANES - rows, ck), BF16)], axis=0)
        car_ref[...] = cr

    def k_side(g):
        k_weights(g, [k_logits(g, s) for s in range(n_sub)])

    def v_side(g, partials):
        tot = partials[0]
        for pv in partials[1:]:
            tot = tot + pv
        acct_ref[(g // n_chunks) % 2] += tot

    def step(g):
        has_k = g < total
        has_v = g >= 1

        @pl.when(has_k)
        def _():
            @pl.when(g % n_chunks == 0)
            def _():
                begin_sequence(g // n_chunks)

            for cp in chunk_copies(g, 0):
                cp.wait()

            @pl.when(g + n_buf - 1 < total)
            def _():
                for cp in chunk_copies(g + n_buf - 1, 0):
                    cp.start()

        @pl.when(has_v)
        def _():
            for cp in chunk_copies(g - 1, 1):
                cp.wait()

        @pl.when(jnp.logical_not(has_v))
        def _():
            k_side(g)

        @pl.when(jnp.logical_and(has_k, has_v))
        def _():
            partials, zs = [], []
            for s in range(n_sub):
                partials.append(v_partial(g - 1, s))
                zs.append(k_logits(g, s))
            v_side(g - 1, partials)
            k_weights(g, zs)

        @pl.when(jnp.logical_not(has_k))
        def _():
            v_side(g - 1, [v_partial(g - 1, s) for s in range(n_sub)])

        @pl.when(g + n_buf - 1 < total)
        def _():
            for cp in chunk_copies(g + n_buf - 1, 1):
                cp.start()

        @pl.when(jnp.logical_and(has_v, (g - 1) % n_chunks == n_chunks - 1))
        def _():
            end_sequence((g - 1) // n_chunks)

    return prime, step


def _attn_kernel(pt_ref, q_ref, k_ref, v_ref, qs_ref, kn_ref, vn_ref, bias_ref, ck_hbm, cv_hbm,
                 o_ref, os_ref,
                 acc_ref, car_ref, z_ref, kbuf, vbuf, sem, qbd_ref, sa_ref, sacc_ref, sacct_ref,
                 scar_ref, cnt_ref, *, blk, n_h, units_per_step, sample_cfg):
    b = pl.program_id(0)
    i = pl.program_id(1)
    prime, sample_step = _sample_attention(
        pt_ref, qs_ref, kn_ref, vn_ref, bias_ref, ck_hbm, cv_hbm, os_ref,
        kbuf, vbuf, sem, qbd_ref, sa_ref, sacc_ref, sacct_ref, scar_ref, **sample_cfg)
    n_sample_steps = sample_cfg["n_seq"] * sample_cfg["n_chunks"] + 1

    @pl.when(jnp.logical_and(b == 0, i == 0))
    def _():
        cnt_ref[0] = 0
        prime()

    def tick(k):
        assert k <= units_per_step
        u = cnt_ref[0]
        cnt_ref[0] = u + k
        g = u // units_per_step

        @pl.when(jnp.logical_and((u + k) // units_per_step > g, g < n_sample_steps))
        def _():
            sample_step(g)

    upper = _strict_upper(blk)
    reps = blk // LANES
    row = lax.broadcasted_iota(jnp.int32, (blk, blk), 0)
    col = lax.broadcasted_iota(jnp.int32, (blk, blk), 1)
    causal = col < row

    def logits(j):
        return [jnp.dot(q_ref[hh], k_ref[hh, j], preferred_element_type=F32) for hh in range(n_h)]

    def tiles(zs, j, mask, z_next_ref):
        heads = range(n_h)
        nxt = logits(jnp.maximum(j - 1, 0))
        for hh in heads:
            z_next_ref[hh] = nxt[hh]
        sps = [_softplus2(z) for z in zs]
        if mask is not None:
            sps = [jnp.where(mask, sp, 0.0) for sp in sps]
        lbs = [z - sp for z, sp in zip(zs, sps)]
        later_all = jnp.dot(jnp.concatenate([sp.astype(BF16) for sp in sps], axis=0), upper,
                            preferred_element_type=F32)
        laters = [later_all[hh * blk:(hh + 1) * blk] for hh in heads]
        cars = [car_ref[hh] for hh in heads]
        for hh in heads:
            car_ref[hh] = cars[hh] + jnp.sum(sps[hh], axis=-1, keepdims=True)
        a_s = [jnp.exp2(lbs[hh] - laters[hh] - jnp.tile(cars[hh], (1, reps))) for hh in heads]
        if mask is not None:
            a_s = [jnp.where(mask, a, 0.0) for a in a_s]
        for p in range(n_h // 2):
            v = v_ref[pl.ds(pl.multiple_of(j * blk, blk), blk), p * LANES:(p + 1) * LANES]
            a_pair = jnp.concatenate([a_s[2 * p].astype(BF16), a_s[2 * p + 1].astype(BF16)], axis=0)
            av = jnp.dot(a_pair, v, preferred_element_type=F32)
            acc_ref[2 * p] += av[:blk]
            acc_ref[2 * p + 1] += av[blk:]

    acc_ref[...] = jnp.zeros_like(acc_ref)
    car_ref[...] = jnp.zeros_like(car_ref)
    tiles(logits(i), i, causal, z_ref)
    tick(1)

    def body(n, c):
        tiles([z_ref[hh] for hh in range(n_h)], i - 1 - n, None, z_ref)
        tick(1)
        return c

    lax.fori_loop(0, i, body, 0)

    lo = lax.broadcasted_iota(jnp.int32, (blk, LANES), 1) < HEAD_DIM
    for p in range(n_h // 2):
        o_ref[:, p * LANES:(p + 1) * LANES] = jnp.where(
            lo, acc_ref[2 * p], acc_ref[2 * p + 1]).astype(o_ref.dtype)


def _attn(qh, kh, vb, page_table, q_nat, k_new, v_new, bias_rows, cache_kt, cache_vt):
    n_batch, n_heads, t_len, _ = qh.shape
    blk = kh.shape[-1]
    n_blk = t_len // blk
    d_sb = n_heads * HEAD_DIM
    n_seq, n_tok, _ = q_nat.shape
    n_pages = page_table.shape[1]
    page = cache_kt.shape[2]
    pages_per_chunk = SAMPLE_CHUNK_PAGES
    n_buf = SAMPLE_RING_SLOTS
    n_chunks = n_pages // pages_per_chunk
    assert n_chunks * pages_per_chunk == n_pages
    rows = n_tok * n_heads
    ck = pages_per_chunk * page
    units = n_batch * n_blk * (n_blk + 1) // 2
    units_per_step = units // (n_seq * n_chunks + 1)
    assert units_per_step >= 1, "not enough prompt work to pace the sample steps"
    sample_cfg = dict(n_seq=n_seq, n_tok=n_tok, n_heads=n_heads, pages_per_chunk=pages_per_chunk,
                      n_chunks=n_chunks, n_buf=n_buf, page=page, sub=blk)
    whole = lambda shape: pl.BlockSpec(shape, lambda b, i, pt: (0,) * len(shape))
    grid_spec = pltpu.PrefetchScalarGridSpec(
        num_scalar_prefetch=1,
        grid=(n_batch, n_blk),
        in_specs=[pl.BlockSpec((None, n_heads, blk, LANES), lambda b, i, pt: (b, 0, i, 0)),
                  pl.BlockSpec((None, n_heads, n_blk, LANES, blk), lambda b, i, pt: (b, 0, 0, 0, 0),
                               pipeline_mode=pl.Buffered(1)),
                  pl.BlockSpec((None, t_len, d_sb), lambda b, i, pt: (b, 0, 0),
                               pipeline_mode=pl.Buffered(1)),
                  whole((n_seq, n_tok, d_sb)), whole((n_seq, n_tok, d_sb)),
                  whole((n_seq, n_tok, d_sb)), whole((rows, 1)),
                  pl.BlockSpec(memory_space=pl.ANY),
                  pl.BlockSpec(memory_space=pl.ANY)],
        out_specs=[pl.BlockSpec((None, blk, d_sb), lambda b, i, pt: (b, i, 0)),
                   whole((n_seq, n_tok, d_sb))],
        scratch_shapes=[pltpu.VMEM((n_heads, blk, LANES), F32),
                        pltpu.VMEM((n_heads, blk, LANES), F32),
                        pltpu.VMEM((n_heads, blk, blk), F32),
                        pltpu.VMEM((n_buf, d_sb, ck), F32),
                        pltpu.VMEM((n_buf, d_sb, ck), F32),
                        pltpu.SemaphoreType.DMA((2, n_buf)),
                        pltpu.VMEM((rows, d_sb), BF16),
                        pltpu.VMEM((2, LANES, ck), BF16),
                        pltpu.VMEM((2, rows, d_sb), F32),
                        pltpu.VMEM((2, d_sb, LANES), F32),
                        pltpu.VMEM((rows, LANES), F32),
                        pltpu.SMEM((1,), jnp.int32)],
    )
    kern = functools.partial(_attn_kernel, blk=blk, n_h=n_heads, units_per_step=units_per_step,
                             sample_cfg=sample_cfg)
    return pl.pallas_call(
        kern, grid_spec=grid_spec,
        out_shape=[jax.ShapeDtypeStruct((n_batch, t_len, d_sb), BF16),
                   jax.ShapeDtypeStruct((n_seq, n_tok, d_sb), F32)],
        compiler_params=_cparams(("arbitrary", "arbitrary")),
        name="attn",
    )(page_table, qh, kh, vb, q_nat, k_new, v_new, bias_rows, cache_kt, cache_vt)


def _outffn_kernel(*refs, tm, d_ff, fc, seq_rows, tiles_per_batch, has_state):
    if has_state:
        (x_ref, osb_ref, oml_ref, g1_ref, sh_ref, sc_ref, g2_ref, n2_ref, wout_ref, wup_ref,
         cw_ref, cb_ref, wdn_ref, p1_ref, p2_ref, y_ref, up_ref) = refs
    else:
        (x_ref, osb_ref, oml_ref, g1_ref, sh_ref, sc_ref, g2_ref, n2_ref, wout_ref, wup_ref,
         cw_ref, cb_ref, wdn_ref, y_ref, st_ref, carry_ref) = refs
    d_sb = osb_ref.shape[1]
    mix = (jnp.dot(osb_ref[...].astype(BF16), wout_ref[0:d_sb, :], preferred_element_type=F32)
           + jnp.dot(oml_ref[...], wout_ref[d_sb:, :], preferred_element_type=F32))
    x1 = x_ref[...] + g1_ref[...] * mix
    ms = jnp.mean(x1 * x1, axis=-1, keepdims=True)
    h2 = x1 * lax.rsqrt(ms + EPS) * n2_ref[...]
    h2 = (h2 * (1.0 + sc_ref[...]) + sh_ref[...]).astype(BF16)

    row = lax.broadcasted_iota(jnp.int32, (tm, 1), 0)
    if has_state:
        t_in = row % seq_rows
    else:
        t_in = row
        first = (pl.program_id(0) % tiles_per_batch) == 0

        @pl.when(first)
        def _():
            carry_ref[...] = jnp.zeros_like(carry_ref)

    def conv(u, cols):
        r1 = pltpu.roll(u, 1, axis=0)
        r2 = pltpu.roll(u, 2, axis=0)
        if has_state:
            up_ref[:, cols] = u
            prev1 = jnp.where(t_in >= 1, r1, p1_ref[:, cols])
            prev2 = jnp.where(t_in >= 2, r2, p2_ref[:, cols])
        else:
            c6 = carry_ref[CARRY_ROWS - 2:CARRY_ROWS - 1, cols]
            c7 = carry_ref[CARRY_ROWS - 1:CARRY_ROWS, cols]
            prev1 = jnp.where(t_in >= 1, r1, c7)
            prev2 = jnp.where(t_in >= 2, r2, jnp.where(t_in == 0, c6, c7))
            carry_ref[:, cols] = u[tm - CARRY_ROWS:, :]
        return (cb_ref[:, cols] + cw_ref[0:1, cols] * prev2 + cw_ref[1:2, cols] * prev1
                + cw_ref[2:3, cols] * u)

    acc = jnp.zeros((tm, x1.shape[1]), F32)
    for f in range(d_ff // fc):
        gc = slice(f * fc, (f + 1) * fc)
        vc = slice(d_ff + f * fc, d_ff + (f + 1) * fc)
        cg = conv(jnp.dot(h2, wup_ref[:, gc], preferred_element_type=F32), gc)
        cv = conv(jnp.dot(h2, wup_ref[:, vc], preferred_element_type=F32), vc)
        act = (cg * jax.nn.sigmoid(cg) * cv).astype(BF16)
        acc = acc + jnp.dot(act, wdn_ref[gc, :], preferred_element_type=F32)
    y_ref[...] = x1 + g2_ref[...] * acc

    if not has_state:
        @pl.when((pl.program_id(0) % tiles_per_batch) == tiles_per_batch - 1)
        def _():
            st_ref[...] = carry_ref[...]


def _outffn(x2, osb, oml, mod, per_row, n_batch, seq_rows, tm, n2, wout_b, wup_b, conv_w, conv_b,
            wdn_b, state_rows):
    rows, d = x2.shape
    d_ff = wdn_b.shape[0]
    d_sb = osb.shape[1]
    d_mlp = oml.shape[1]
    tiles = rows // tm
    tpb = max(tiles // n_batch, 1)
    has_state = state_rows is not None
    const = lambda i: (0, 0)
    row_spec = lambda w: pl.BlockSpec((tm, w), lambda i: (i, 0))
    in_specs = [
        row_spec(d), row_spec(d_sb), row_spec(d_mlp),
        _mod_spec(per_row, 2, tm, d, tpb),
        _mod_spec(per_row, 3, tm, d, tpb),
        _mod_spec(per_row, 4, tm, d, tpb),
        _mod_spec(per_row, 5, tm, d, tpb),
        pl.BlockSpec((1, d), const),
        _resident(wout_b.shape),
        _resident(wup_b.shape),
        pl.BlockSpec(conv_w.shape, const),
        pl.BlockSpec((1, 2 * d_ff), const),
        _resident(wdn_b.shape),
    ]
    args = [x2, osb, oml, mod, mod, mod, mod, n2, wout_b, wup_b, conv_w, conv_b, wdn_b]
    if has_state:
        in_specs += [_resident((rows, 2 * d_ff)), _resident((rows, 2 * d_ff))]
        args += list(state_rows)
        out_specs = [row_spec(d), row_spec(2 * d_ff)]
        out_shape = [jax.ShapeDtypeStruct((rows, d), F32),
                     jax.ShapeDtypeStruct((rows, 2 * d_ff), F32)]
        scratch = []
    else:
        out_specs = [row_spec(d),
                     pl.BlockSpec((None, CARRY_ROWS, 2 * d_ff), lambda i: (i // tpb, 0, 0))]
        out_shape = [jax.ShapeDtypeStruct((rows, d), F32),
                     jax.ShapeDtypeStruct((n_batch, CARRY_ROWS, 2 * d_ff), F32)]
        scratch = [pltpu.VMEM((CARRY_ROWS, 2 * d_ff), F32)]
    kern = functools.partial(_outffn_kernel, tm=tm, d_ff=d_ff, fc=d_ff // 2, seq_rows=seq_rows,
                             tiles_per_batch=tpb, has_state=has_state)
    return pl.pallas_call(
        kern, grid=(tiles,), in_specs=in_specs, out_specs=out_specs, out_shape=out_shape,
        scratch_shapes=scratch,
        compiler_params=_cparams(("arbitrary",)),
        name="outffn_sample" if has_state else "outffn_prompt",
    )(*args)


def _layer(l, yp, ys, c_prompt, c_sample, cache_k, cache_v, state_conv, page_table,
           norm1_g, norm2_g, w_ada, b_ada, w_in, q_norm_g, k_norm_g, sb_bias, v_norm_g,
           w_s, b_s, w_out, w_up, conv_w, conv_b, w_down):
    bp, tp, d = yp.shape
    db, ts, _ = ys.shape
    d_ff = w_down.shape[1]
    n_heads = sb_bias.shape[1]
    d_sb = n_heads * HEAD_DIM
    n_grp = v_norm_g.shape[1]
    d_mlp = n_grp * CHUNK
    n_pool, page = cache_k.shape[1], cache_k.shape[2]

    n_c = bp + db
    pad = (-n_c) % 8
    c_all = jnp.concatenate([c_prompt, c_sample, jnp.zeros((pad, d), F32)], axis=0)
    mod = _ada(c_all, w_ada[l], b_ada[l])
    mod_p = mod[:bp].reshape(bp, N_MOD, 1, d)
    mod_s = jnp.repeat(mod[bp:n_c].reshape(db, N_MOD, d).transpose(1, 0, 2), ts, axis=1)

    win_b = w_in[l].astype(BF16)
    wnat_b = jnp.concatenate([win_b[:, :d_sb], win_b[:, 2 * d_sb:]], axis=1)
    wkvt_b = win_b[:, d_sb:3 * d_sb].T
    wout_b = w_out[l].astype(BF16)
    wup_b = w_up[l].astype(BF16)
    wdn_b = w_down[l].astype(BF16)
    g1 = norm1_g[l].reshape(1, d)
    n2 = norm2_g[l].reshape(1, d)
    gq = (jnp.tile(q_norm_g[l], n_heads) * (LOG2E * HEAD_DIM ** -0.5)).reshape(1, d_sb)
    sbb2 = sb_bias[l] * LOG2E
    gk = jnp.tile(k_norm_g[l], n_heads).reshape(1, d_sb)
    gkc = jnp.broadcast_to(k_norm_g[l][:, None], (HEAD_DIM, LANES))
    gv = v_norm_g[l].reshape(1, d_mlp)
    qaux = jnp.where((jnp.arange(LANES) == HEAD_DIM)[None, :], sbb2[:, None], 0.0).astype(F32)
    cb = conv_b[l].reshape(1, 2 * d_ff)

    bmix_p = jnp.repeat(b_s[l].T, CHUNK, axis=1)
    xp2 = yp.reshape(bp * tp, d)
    kt, vt, qh, kh, vb, oml_p = _inproj_prompt(
        xp2, mod_p, bp, PROMPT_ROWS, wnat_b, wkvt_b, g1, gq, gkc, gv, qaux, w_s[l], bmix_p,
        d_sb, d_mlp)

    rows_s = db * ts
    bmix_s = jnp.repeat(jnp.tile(b_s[l][:, :ts], (1, CHUNK // ts)).T, CHUNK, axis=1)
    xs2 = ys.reshape(rows_s, d)
    ks_, vs_, qn_s, gv_s, oml_s = _inproj_sample(
        xs2, mod_s, ts, win_b, g1, gq, gk, gv, w_s[l], bmix_s, d_sb, d_mlp)
    bias_rows = jnp.tile(sbb2, ts).reshape(ts * n_heads, 1)
    cache_kt = cache_k[l].transpose(0, 2, 3, 1).reshape(n_pool, d_sb, page)
    cache_vt = cache_v[l].transpose(0, 2, 3, 1).reshape(n_pool, d_sb, page)
    osb_p, osb_s = _attn(qh, kh, vb.reshape(bp, tp, d_sb), page_table,
                         qn_s.reshape(db, ts, d_sb), ks_.reshape(db, ts, d_sb),
                         vs_.reshape(db, ts, d_sb), bias_rows, cache_kt, cache_vt)
    yp2, st_p = _outffn(xp2, osb_p.reshape(bp * tp, d_sb), oml_p, mod_p, False, bp, tp,
                        PROMPT_ROWS, n2,
                        wout_b, wup_b, conv_w[l], cb, wdn_b, None)
    st = state_conv[l]
    p1 = jnp.concatenate([st[:, 1:2], jnp.zeros((db, ts - 1, 2 * d_ff), F32)], axis=1)
    p2 = jnp.concatenate([st, jnp.zeros((db, ts - 2, 2 * d_ff), F32)], axis=1)
    ys2, up_s = _outffn(xs2, osb_s.reshape(rows_s, d_sb), oml_s, mod_s, True, db, ts, rows_s, n2,
                        wout_b, wup_b, conv_w[l], cb, wdn_b,
                        (p1.reshape(rows_s, 2 * d_ff), p2.reshape(rows_s, 2 * d_ff)))

    outs = (kt.reshape(bp, n_heads, HEAD_DIM, tp).transpose(0, 3, 1, 2),
            vt.reshape(bp, n_heads, HEAD_DIM, tp).transpose(0, 3, 1, 2),
            ks_.reshape(db, ts, n_heads, HEAD_DIM), vs_.reshape(db, ts, n_heads, HEAD_DIM),
            gv_s.reshape(db, ts, n_grp, CHUNK),
            st_p[:, CARRY_ROWS - (CONV_W - 1):],
            up_s.reshape(db, ts, 2 * d_ff)[:, ts - (CONV_W - 1):])
    return yp2.reshape(bp, tp, d), ys2.reshape(db, ts, d), outs


def kernel(x_prompt, x_sample, c_prompt, c_sample, cache_k, cache_v, state_conv, page_table,
           norm1_g, norm2_g, w_ada, b_ada, w_in, q_norm_g, k_norm_g, sb_bias, v_norm_g,
           w_s, b_s, w_out, w_up, conv_w, conv_b, w_down):
    depth = w_in.shape[0]
    yp, ys = x_prompt, x_sample
    per_layer = []
    for l in range(depth):
        yp, ys, outs = _layer(l, yp, ys, c_prompt, c_sample, cache_k, cache_v, state_conv,
                              page_table, norm1_g, norm2_g, w_ada, b_ada, w_in, q_norm_g,
                              k_norm_g, sb_bias, v_norm_g, w_s, b_s, w_out, w_up, conv_w,
                              conv_b, w_down)
        per_layer.append(outs)
    stacked = tuple(jnp.stack([o[i] for o in per_layer]) for i in range(7))
    return (yp, ys) + stacked
```

```python
import functools

import jax
import jax.numpy as jnp
from jax import lax
from jax.experimental import pallas as pl
from jax.experimental.pallas import tpu as pltpu

F32 = jnp.float32
BF16 = jnp.bfloat16

EPS = 1e-6
LOG2E = 1.4426950408889634
HEAD_DIM = 64
LANES = 128
CHUNK = 128
N_MOD = 6
CONV_W = 3
CARRY_ROWS = 8
ATTN_BLK = 256
PROMPT_ROWS = 512
SAMPLE_CHUNK_PAGES = 16
SAMPLE_RING_SLOTS = 3

VMEM_LIMIT = 56 * 1024 * 1024

NT_DIMS = (((1,), (1,)), ((), ()))


def _cparams(sem):
    return pltpu.CompilerParams(dimension_semantics=sem, vmem_limit_bytes=VMEM_LIMIT)


def _resident(shape):
    nd = len(shape)
    return pl.BlockSpec(shape, lambda *_: (0,) * nd, pipeline_mode=pl.Buffered(1))


def _mod_spec(per_row, k, tm, d, tiles_per_batch):
    if per_row:
        return pl.BlockSpec((None, tm, d), lambda i: (k, i, 0))
    return pl.BlockSpec((None, None, 1, d), lambda i: (i // tiles_per_batch, k, 0, 0))


def _ada_kernel(c_ref, w_ref, b_ref, o_ref):
    c = c_ref[...]
    s = c * jax.nn.sigmoid(c)
    o_ref[...] = jnp.dot(s, w_ref[...], preferred_element_type=F32,
                         precision=lax.Precision.HIGHEST) + b_ref[...]


def _ada(c_all, w_ada, b_ada):
    rows, d = c_all.shape
    n = w_ada.shape[1]
    bn = n // 4
    return pl.pallas_call(
        _ada_kernel,
        grid=(n // bn,),
        in_specs=[pl.BlockSpec((rows, d), lambda j: (0, 0)),
                  pl.BlockSpec((d, bn), lambda j: (0, j)),
                  pl.BlockSpec((1, bn), lambda j: (0, j))],
        out_specs=pl.BlockSpec((rows, bn), lambda j: (0, j)),
        out_shape=jax.ShapeDtypeStruct((rows, n), F32),
        compiler_params=_cparams(("arbitrary",)),
        name="ada",
    )(c_all, w_ada, b_ada.reshape(1, n))


def _two_head_norm(blk, g):
    lo = lax.broadcasted_iota(jnp.int32, blk.shape, 1) < HEAD_DIM
    sq = blk * blk
    s_lo = jnp.sum(jnp.where(lo, sq, 0.0), axis=-1, keepdims=True)
    s_hi = jnp.sum(jnp.where(lo, 0.0, sq), axis=-1, keepdims=True)
    r = lax.rsqrt(jnp.where(lo, s_lo, s_hi) * (1.0 / HEAD_DIM) + EPS)
    return blk * r * g


def _spatial_gate(proj, u0, g0, gv_ref, wmix_ref, bmix_ref, om_ref, gvo_ref, tm, n_grp, seq_rows):
    r_i = lax.broadcasted_iota(jnp.int32, (CHUNK, CHUNK), 0)
    c_i = lax.broadcasted_iota(jnp.int32, (CHUNK, CHUNK), 1)
    if seq_rows >= CHUNK:
        keep = r_i >= c_i
        mixing = lambda w: w
    else:
        keep = jnp.logical_and(r_i >= c_i, c_i >= (r_i // seq_rows) * seq_rows)
        pick = jnp.where(c_i == r_i % seq_rows, 1.0, 0.0)

        def mixing(w):
            rows_picked = jnp.dot(pick, w, preferred_element_type=F32,
                                  precision=lax.Precision.HIGHEST)
            return lax.dot_general(rows_picked, pick, NT_DIMS, preferred_element_type=F32,
                                   precision=lax.Precision.HIGHEST)
    for g in range(n_grp):
        cs = slice(g * CHUNK, (g + 1) * CHUNK)
        wm = jnp.where(keep, mixing(wmix_ref[g]), 0.0).astype(BF16)
        u = jax.nn.gelu(proj[:, u0 + g * CHUNK:u0 + (g + 1) * CHUNK])
        vg = jax.nn.gelu(proj[:, g0 + g * CHUNK:g0 + (g + 1) * CHUNK])
        vn = vg * lax.rsqrt(jnp.mean(vg * vg, axis=-1, keepdims=True) + EPS) * gv_ref[:, cs]
        if gvo_ref is not None:
            gvo_ref[:, cs] = vn
        vnb = vn.astype(BF16)
        for c in range(tm // CHUNK):
            rs = slice(c * CHUNK, (c + 1) * CHUNK)
            mixed = jnp.dot(wm, vnb[rs], preferred_element_type=F32) + bmix_ref[:, cs]
            om_ref[rs, cs] = (u[rs] * mixed).astype(BF16)


def _normed_input(x_ref, sh_ref, sc_ref, g_ref):
    x = x_ref[...]
    ms = jnp.mean(x * x, axis=-1, keepdims=True)
    h = x * lax.rsqrt(ms + EPS) * g_ref[...]
    return (h * (1.0 + sc_ref[...]) + sh_ref[...]).astype(BF16)


def _inproj_prompt_kernel(x_ref, sh_ref, sc_ref, g1_ref, wnat_ref, wkvt_ref, gq_ref, gkc_ref,
                          gv_ref, qaux_ref, wmix_ref, bmix_ref,
                          kt_ref, vt_ref, qh_ref, kh_ref, vb_ref, om_ref,
                          *, tm, d_sb, d_mlp, seq_rows):
    n_heads = d_sb // HEAD_DIM
    hb = _normed_input(x_ref, sh_ref, sc_ref, g1_ref)
    proj = jnp.dot(hb, wnat_ref[...], preferred_element_type=F32)
    kvt = lax.dot_general(wkvt_ref[...], hb, NT_DIMS, preferred_element_type=F32)

    gkc = jnp.tile(gkc_ref[...], (1, tm // LANES))
    kb = kh_ref.shape[-1]
    aux_rows = jnp.where(lax.broadcasted_iota(jnp.int32, (HEAD_DIM, kb), 0) == 0, 1.0, 0.0)
    for hd in range(n_heads):
        rs = slice(hd * HEAD_DIM, (hd + 1) * HEAD_DIM)
        blk = kvt[rs, :]
        kn = blk * lax.rsqrt(jnp.mean(blk * blk, axis=0, keepdims=True) + EPS) * gkc
        kt_ref[rs, :] = kn
        for c in range(tm // kb):
            kh_ref[hd, c, 0:HEAD_DIM, :] = kn[:, c * kb:(c + 1) * kb].astype(BF16)
            kh_ref[hd, c, HEAD_DIM:, :] = aux_rows.astype(BF16)
    vt_ref[...] = kvt[d_sb:, :]

    lo = lax.broadcasted_iota(jnp.int32, (tm, LANES), 1) < HEAD_DIM
    for p in range(d_sb // LANES):
        cs = slice(p * LANES, (p + 1) * LANES)
        qn = _two_head_norm(proj[:, p * LANES:(p + 1) * LANES], gq_ref[:, cs])
        vb_ref[:, cs] = proj[:, d_sb + p * LANES:d_sb + (p + 1) * LANES].astype(BF16)
        qh_ref[2 * p] = jnp.where(lo, qn, qaux_ref[2 * p:2 * p + 1, :]).astype(BF16)
        qr = pltpu.roll(qn, HEAD_DIM, axis=1)
        qh_ref[2 * p + 1] = jnp.where(lo, qr, qaux_ref[2 * p + 1:2 * p + 2, :]).astype(BF16)

    _spatial_gate(proj, 2 * d_sb, 2 * d_sb + d_mlp, gv_ref, wmix_ref, bmix_ref, om_ref, None,
                  tm, d_mlp // CHUNK, seq_rows)


def _inproj_sample_kernel(x_ref, sh_ref, sc_ref, g1_ref, win_ref, gq_ref, gk_ref, gv_ref,
                          wmix_ref, bmix_ref, k_ref, v_ref, qn_ref, gvo_ref, om_ref,
                          *, tm, d_sb, d_mlp, seq_rows):
    hb = _normed_input(x_ref, sh_ref, sc_ref, g1_ref)
    proj = jnp.dot(hb, win_ref[...], preferred_element_type=F32)
    for p in range(d_sb // LANES):
        cs = slice(p * LANES, (p + 1) * LANES)
        qn_ref[:, cs] = _two_head_norm(proj[:, p * LANES:(p + 1) * LANES], gq_ref[:, cs])
        k_ref[:, cs] = _two_head_norm(proj[:, d_sb + p * LANES:d_sb + (p + 1) * LANES],
                                      gk_ref[:, cs])
        v_ref[:, cs] = proj[:, 2 * d_sb + p * LANES:2 * d_sb + (p + 1) * LANES]
    _spatial_gate(proj, 3 * d_sb, 3 * d_sb + d_mlp, gv_ref, wmix_ref, bmix_ref, om_ref, gvo_ref,
                  tm, d_mlp // CHUNK, seq_rows)


def _inproj_prompt(x2, mod, n_batch, tm, wnat_b, wkvt_b, g1, gq, gkc, gv, qaux, wmix, bmix,
                   d_sb, d_mlp):
    rows, d = x2.shape
    n_heads = d_sb // HEAD_DIM
    t_len = rows // n_batch
    tpb = t_len // tm
    const = lambda i: (0, 0)
    in_specs = [
        pl.BlockSpec((tm, d), lambda i: (i, 0)),
        _mod_spec(False, 0, tm, d, tpb),
        _mod_spec(False, 1, tm, d, tpb),
        pl.BlockSpec((1, d), const),
        _resident(wnat_b.shape),
        _resident(wkvt_b.shape),
        pl.BlockSpec((1, d_sb), const),
        pl.BlockSpec((HEAD_DIM, LANES), const),
        pl.BlockSpec((1, d_mlp), const),
        pl.BlockSpec((n_heads, LANES), const),
        pl.BlockSpec(wmix.shape, lambda i: (0, 0, 0)),
        pl.BlockSpec((CHUNK, d_mlp), const),
    ]
    row_spec = lambda w: pl.BlockSpec((tm, w), lambda i: (i, 0))
    tmin_spec = pl.BlockSpec((None, d_sb, tm), lambda i: (i // tpb, 0, i % tpb))
    out_specs = [
        tmin_spec, tmin_spec,
        pl.BlockSpec((None, n_heads, tm, LANES), lambda i: (i // tpb, 0, i % tpb, 0)),
        pl.BlockSpec((None, n_heads, tm // ATTN_BLK, LANES, ATTN_BLK),
                     lambda i: (i // tpb, 0, i % tpb, 0, 0)),
        row_spec(d_sb), row_spec(d_mlp)]
    out_shape = [
        jax.ShapeDtypeStruct((n_batch, d_sb, t_len), F32),
        jax.ShapeDtypeStruct((n_batch, d_sb, t_len), F32),
        jax.ShapeDtypeStruct((n_batch, n_heads, t_len, LANES), BF16),
        jax.ShapeDtypeStruct((n_batch, n_heads, t_len // ATTN_BLK, LANES, ATTN_BLK), BF16),
        jax.ShapeDtypeStruct((rows, d_sb), BF16),
        jax.ShapeDtypeStruct((rows, d_mlp), BF16)]
    kern = functools.partial(_inproj_prompt_kernel, tm=tm, d_sb=d_sb, d_mlp=d_mlp, seq_rows=t_len)
    return pl.pallas_call(
        kern, grid=(rows // tm,), in_specs=in_specs, out_specs=out_specs, out_shape=out_shape,
        compiler_params=_cparams(("arbitrary",)), name="inproj_prompt",
    )(x2, mod, mod, g1, wnat_b, wkvt_b, gq, gkc, gv, qaux, wmix, bmix)


def _inproj_sample(x2, mod, seq_rows, win_b, g1, gq, gk, gv, wmix, bmix, d_sb, d_mlp):
    rows, d = x2.shape
    tm = rows
    const = lambda i: (0, 0)
    in_specs = [
        pl.BlockSpec((tm, d), lambda i: (i, 0)),
        _mod_spec(True, 0, tm, d, 1),
        _mod_spec(True, 1, tm, d, 1),
        pl.BlockSpec((1, d), const),
        _resident(win_b.shape),
        pl.BlockSpec((1, d_sb), const),
        pl.BlockSpec((1, d_sb), const),
        pl.BlockSpec((1, d_mlp), const),
        pl.BlockSpec(wmix.shape, lambda i: (0, 0, 0)),
        pl.BlockSpec((CHUNK, d_mlp), const),
    ]
    row_spec = lambda w: pl.BlockSpec((tm, w), lambda i: (i, 0))
    out_specs = [row_spec(d_sb), row_spec(d_sb), row_spec(d_sb), row_spec(d_mlp), row_spec(d_mlp)]
    out_shape = [jax.ShapeDtypeStruct((rows, d_sb), F32),
                 jax.ShapeDtypeStruct((rows, d_sb), F32),
                 jax.ShapeDtypeStruct((rows, d_sb), F32),
                 jax.ShapeDtypeStruct((rows, d_mlp), F32),
                 jax.ShapeDtypeStruct((rows, d_mlp), BF16)]
    kern = functools.partial(_inproj_sample_kernel, tm=tm, d_sb=d_sb, d_mlp=d_mlp,
                             seq_rows=seq_rows)
    return pl.pallas_call(
        kern, grid=(1,), in_specs=in_specs, out_specs=out_specs, out_shape=out_shape,
        compiler_params=_cparams(("arbitrary",)), name="inproj_sample",
    )(x2, mod, mod, g1, win_b, gq, gk, gv, wmix, bmix)


def _softplus2(z2):
    return jnp.maximum(z2, 0.0) + jnp.log(1.0 + jnp.exp2(-jnp.abs(z2))) * LOG2E


def _strict_upper(n):
    r = lax.broadcasted_iota(jnp.int32, (n, n), 0)
    c = lax.broadcasted_iota(jnp.int32, (n, n), 1)
    return jnp.where(r > c, 1.0, 0.0).astype(BF16)


def _sample_attention(pt_ref, q_ref, kn_ref, vn_ref, bias_ref, ck_hbm, cv_hbm, o_ref,
                      kbuf, vbuf, sem, qbd_ref, a_ref, acc_ref, acct_ref, car_ref,
                      *, n_seq, n_tok, n_heads, pages_per_chunk, n_chunks, n_buf, page, sub):
    total = n_seq * n_chunks
    d_sb = n_heads * HEAD_DIM
    rows = n_tok * n_heads
    ck = pages_per_chunk * page
    n_sub = ck // sub

    def chunk_copies(g, which):
        seq = g // n_chunks
        chunk = n_chunks - 1 - g % n_chunks
        slot = g % n_buf
        src, dst = ((ck_hbm, kbuf), (cv_hbm, vbuf))[which]
        cps = []
        for pg in range(pages_per_chunk):
            pid = pt_ref[seq, chunk * pages_per_chunk + pg]
            cols = pl.ds(pg * page, page)
            cps.append(pltpu.make_async_copy(src.at[pid], dst.at[slot, :, cols], sem.at[which, slot]))
        return cps

    def prime():
        for g0 in range(n_buf - 1):
            for cp in chunk_copies(g0, 0) + chunk_copies(g0, 1):
                cp.start()

    def own_mask():
        r_i = lax.broadcasted_iota(jnp.int32, (rows, d_sb), 0)
        c_i = lax.broadcasted_iota(jnp.int32, (rows, d_sb), 1)
        return (r_i % n_heads) == (c_i // HEAD_DIM)

    def begin_sequence(seq):
        q_nat = q_ref[seq]
        q_rep = jnp.concatenate(
            [jnp.broadcast_to(q_nat[t:t + 1, :], (n_heads, d_sb)) for t in range(n_tok)], axis=0)
        q_bd = jnp.where(own_mask(), q_rep, 0.0)
        qbd_ref[...] = q_bd.astype(BF16)
        bias = bias_ref[...]
        tok = lax.broadcasted_iota(jnp.int32, (rows, 1), 0) // n_heads
        acc = jnp.zeros((rows, d_sb), F32)
        car = jnp.zeros((rows, 1), F32)
        k_new = kn_ref[seq]
        v_new = vn_ref[seq]
        for s in range(n_tok - 1, -1, -1):
            z = jnp.sum(q_bd * k_new[s:s + 1, :], axis=-1, keepdims=True) + bias
            vis = tok > s
            sp = jnp.where(vis, _softplus2(z), 0.0)
            a = jnp.where(vis, jnp.exp2(z - sp - car), 0.0)
            acc = acc + a * v_new[s:s + 1, :]
            car = car + sp
        acc_ref[seq % 2] = acc
        acct_ref[seq % 2] = jnp.zeros(acct_ref.shape[1:], F32)
        car_ref[...] = jnp.broadcast_to(car, (rows, LANES))

    def end_sequence(seq):
        res = jnp.where(own_mask(), acc_ref[seq % 2] + acct_ref[seq % 2].T[0:rows, :], 0.0)
        o_ref[seq] = jnp.concatenate(
            [jnp.sum(res[t * n_heads:(t + 1) * n_heads], axis=0, keepdims=True)
             for t in range(n_tok)], axis=0)

    def k_logits(g, s):
        cols = pl.ds(s * sub, sub)
        return jnp.dot(qbd_ref[...], kbuf[g % n_buf, :, cols].astype(BF16),
                       preferred_element_type=F32) + bias_ref[...]

    def v_partial(g, s):
        cols = pl.ds(s * sub, sub)
        return lax.dot_general(vbuf[g % n_buf, :, cols].astype(BF16), a_ref[g % 2, :, cols],
                               NT_DIMS, preferred_element_type=F32)

    def k_weights(g, zs):
        upper = _strict_upper(sub)
        sps = [_softplus2(z) for z in zs]
        sums = [jnp.sum(sp, axis=-1, keepdims=True) for sp in sps]
        cr = car_ref[...]
        a_parts = [None] * n_sub
        later_all = jnp.dot(jnp.concatenate([sp.astype(BF16) for sp in sps], axis=0), upper,
                            preferred_element_type=F32)
        for s in range(n_sub - 1, -1, -1):
            later = later_all[s * rows:(s + 1) * rows]
            t = zs[s] - sps[s] - later - jnp.tile(cr, (1, sub // LANES))
            a_parts[s] = jnp.exp2(t).astype(BF16)
            cr = cr + sums[s]
        a_ref[g % 2] = jnp.concatenate(
            [jnp.concatenate(a_parts, axis=1), jnp.zeros((LANES - rows, ck), BF16)], axis=0)
        car_ref[...] = cr

    def k_side(g):
        k_weights(g, [k_logits(g, s) for s in range(n_sub)])

    def v_side(g, partials):
        tot = partials[0]
        for pv in partials[1:]:
            tot = tot + pv
        acct_ref[(g // n_chunks) % 2] += tot

    def step(g):
        has_k = g < total
        has_v = g >= 1

        @pl.when(has_k)
        def _():
            @pl.when(g % n_chunks == 0)
            def _():
                begin_sequence(g // n_chunks)

            for cp in chunk_copies(g, 0):
                cp.wait()

            @pl.when(g + n_buf - 1 < total)
            def _():
                for cp in chunk_copies(g + n_buf - 1, 0):
                    cp.start()

        @pl.when(has_v)
        def _():
            for cp in chunk_copies(g - 1, 1):
                cp.wait()

        @pl.when(jnp.logical_not(has_v))
        def _():
            k_side(g)

        @pl.when(jnp.logical_and(has_k, has_v))
        def _():
            partials, zs = [], []
            for s in range(n_sub):
                partials.append(v_partial(g - 1, s))
                zs.append(k_logits(g, s))
            v_side(g - 1, partials)
            k_weights(g, zs)

        @pl.when(jnp.logical_not(has_k))
        def _():
            v_side(g - 1, [v_partial(g - 1, s) for s in range(n_sub)])

        @pl.when(g + n_buf - 1 < total)
        def _():
            for cp in chunk_copies(g + n_buf - 1, 1):
                cp.start()

        @pl.when(jnp.logical_and(has_v, (g - 1) % n_chunks == n_chunks - 1))
        def _():
            end_sequence((g - 1) // n_chunks)

    return prime, step


def _attn_kernel(pt_ref, q_ref, k_ref, v_ref, qs_ref, kn_ref, vn_ref, bias_ref, ck_hbm, cv_hbm,
                 o_ref, os_ref,
                 acc_ref, car_ref, z_ref, kbuf, vbuf, sem, qbd_ref, sa_ref, sacc_ref, sacct_ref,
                 scar_ref, cnt_ref, *, blk, n_h, units_per_step, sample_cfg):
    b = pl.program_id(0)
    i = pl.program_id(1)
    prime, sample_step = _sample_attention(
        pt_ref, qs_ref, kn_ref, vn_ref, bias_ref, ck_hbm, cv_hbm, os_ref,
        kbuf, vbuf, sem, qbd_ref, sa_ref, sacc_ref, sacct_ref, scar_ref, **sample_cfg)
    n_sample_steps = sample_cfg["n_seq"] * sample_cfg["n_chunks"] + 1

    @pl.when(jnp.logical_and(b == 0, i == 0))
    def _():
        cnt_ref[0] = 0
        prime()

    def tick(k):
        assert k <= units_per_step
        u = cnt_ref[0]
        cnt_ref[0] = u + k
        g = u // units_per_step

        @pl.when(jnp.logical_and((u + k) // units_per_step > g, g < n_sample_steps))
        def _():
            sample_step(g)

    upper = _strict_upper(blk)
    reps = blk // LANES
    row = lax.broadcasted_iota(jnp.int32, (blk, blk), 0)
    col = lax.broadcasted_iota(jnp.int32, (blk, blk), 1)
    causal = col < row

    def logits(j):
        return [jnp.dot(q_ref[hh], k_ref[hh, j], preferred_element_type=F32) for hh in range(n_h)]

    def tiles(zs, j, mask, z_next_ref):
        heads = range(n_h)
        nxt = logits(jnp.maximum(j - 1, 0))
        for hh in heads:
            z_next_ref[hh] = nxt[hh]
        sps = [_softplus2(z) for z in zs]
        if mask is not None:
            sps = [jnp.where(mask, sp, 0.0) for sp in sps]
        lbs = [z - sp for z, sp in zip(zs, sps)]
        later_all = jnp.dot(jnp.concatenate([sp.astype(BF16) for sp in sps], axis=0), upper,
                            preferred_element_type=F32)
        laters = [later_all[hh * blk:(hh + 1) * blk] for hh in heads]
        cars = [car_ref[hh] for hh in heads]
        for hh in heads:
            car_ref[hh] = cars[hh] + jnp.sum(sps[hh], axis=-1, keepdims=True)
        a_s = [jnp.exp2(lbs[hh] - laters[hh] - jnp.tile(cars[hh], (1, reps))) for hh in heads]
        if mask is not None:
            a_s = [jnp.where(mask, a, 0.0) for a in a_s]
        for p in range(n_h // 2):
            v = v_ref[pl.ds(pl.multiple_of(j * blk, blk), blk), p * LANES:(p + 1) * LANES]
            a_pair = jnp.concatenate([a_s[2 * p].astype(BF16), a_s[2 * p + 1].astype(BF16)], axis=0)
            av = jnp.dot(a_pair, v, preferred_element_type=F32)
            acc_ref[2 * p] += av[:blk]
            acc_ref[2 * p + 1] += av[blk:]

    acc_ref[...] = jnp.zeros_like(acc_ref)
    car_ref[...] = jnp.zeros_like(car_ref)
    tiles(logits(i), i, causal, z_ref)
    tick(1)

    def body(n, c):
        tiles([z_ref[hh] for hh in range(n_h)], i - 1 - n, None, z_ref)
        tick(1)
        return c

    lax.fori_loop(0, i, body, 0)

    lo = lax.broadcasted_iota(jnp.int32, (blk, LANES), 1) < HEAD_DIM
    for p in range(n_h // 2):
        o_ref[:, p * LANES:(p + 1) * LANES] = jnp.where(
            lo, acc_ref[2 * p], acc_ref[2 * p + 1]).astype(o_ref.dtype)


def _attn(qh, kh, vb, page_table, q_nat, k_new, v_new, bias_rows, cache_kt, cache_vt):
    n_batch, n_heads, t_len, _ = qh.shape
    blk = kh.shape[-1]
    n_blk = t_len // blk
    d_sb = n_heads * HEAD_DIM
    n_seq, n_tok, _ = q_nat.shape
    n_pages = page_table.shape[1]
    page = cache_kt.shape[2]
    pages_per_chunk = SAMPLE_CHUNK_PAGES
    n_buf = SAMPLE_RING_SLOTS
    n_chunks = n_pages // pages_per_chunk
    assert n_chunks * pages_per_chunk == n_pages
    rows = n_tok * n_heads
    ck = pages_per_chunk * page
    units = n_batch * n_blk * (n_blk + 1) // 2
    units_per_step = units // (n_seq * n_chunks + 1)
    assert units_per_step >= 1, "not enough prompt work to pace the sample steps"
    sample_cfg = dict(n_seq=n_seq, n_tok=n_tok, n_heads=n_heads, pages_per_chunk=pages_per_chunk,
                      n_chunks=n_chunks, n_buf=n_buf, page=page, sub=blk)
    whole = lambda shape: pl.BlockSpec(shape, lambda b, i, pt: (0,) * len(shape))
    grid_spec = pltpu.PrefetchScalarGridSpec(
        num_scalar_prefetch=1,
        grid=(n_batch, n_blk),
        in_specs=[pl.BlockSpec((None, n_heads, blk, LANES), lambda b, i, pt: (b, 0, i, 0)),
                  pl.BlockSpec((None, n_heads, n_blk, LANES, blk), lambda b, i, pt: (b, 0, 0, 0, 0),
                               pipeline_mode=pl.Buffered(1)),
                  pl.BlockSpec((None, t_len, d_sb), lambda b, i, pt: (b, 0, 0),
                               pipeline_mode=pl.Buffered(1)),
                  whole((n_seq, n_tok, d_sb)), whole((n_seq, n_tok, d_sb)),
                  whole((n_seq, n_tok, d_sb)), whole((rows, 1)),
                  pl.BlockSpec(memory_space=pl.ANY),
                  pl.BlockSpec(memory_space=pl.ANY)],
        out_specs=[pl.BlockSpec((None, blk, d_sb), lambda b, i, pt: (b, i, 0)),
                   whole((n_seq, n_tok, d_sb))],
        scratch_shapes=[pltpu.VMEM((n_heads, blk, LANES), F32),
                        pltpu.VMEM((n_heads, blk, LANES), F32),
                        pltpu.VMEM((n_heads, blk, blk), F32),
                        pltpu.VMEM((n_buf, d_sb, ck), F32),
                        pltpu.VMEM((n_buf, d_sb, ck), F32),
                        pltpu.SemaphoreType.DMA((2, n_buf)),
                        pltpu.VMEM((rows, d_sb), BF16),
                        pltpu.VMEM((2, LANES, ck), BF16),
                        pltpu.VMEM((2, rows, d_sb), F32),
                        pltpu.VMEM((2, d_sb, LANES), F32),
                        pltpu.VMEM((rows, LANES), F32),
                        pltpu.SMEM((1,), jnp.int32)],
    )
    kern = functools.partial(_attn_kernel, blk=blk, n_h=n_heads, units_per_step=units_per_step,
                             sample_cfg=sample_cfg)
    return pl.pallas_call(
        kern, grid_spec=grid_spec,
        out_shape=[jax.ShapeDtypeStruct((n_batch, t_len, d_sb), BF16),
                   jax.ShapeDtypeStruct((n_seq, n_tok, d_sb), F32)],
        compiler_params=_cparams(("arbitrary", "arbitrary")),
        name="attn",
    )(page_table, qh, kh, vb, q_nat, k_new, v_new, bias_rows, cache_kt, cache_vt)


def _outffn_kernel(*refs, tm, d_ff, fc, seq_rows, tiles_per_batch, has_state):
    if has_state:
        (x_ref, osb_ref, oml_ref, g1_ref, sh_ref, sc_ref, g2_ref, n2_ref, wout_ref, wup_ref,
         cw_ref, cb_ref, wdn_ref, p1_ref, p2_ref, y_ref, up_ref) = refs
    else:
        (x_ref, osb_ref, oml_ref, g1_ref, sh_ref, sc_ref, g2_ref, n2_ref, wout_ref, wup_ref,
         cw_ref, cb_ref, wdn_ref, y_ref, st_ref, carry_ref) = refs
    d_sb = osb_ref.shape[1]
    mix = (jnp.dot(osb_ref[...].astype(BF16), wout_ref[0:d_sb, :], preferred_element_type=F32)
           + jnp.dot(oml_ref[...], wout_ref[d_sb:, :], preferred_element_type=F32))
    x1 = x_ref[...] + g1_ref[...] * mix
    ms = jnp.mean(x1 * x1, axis=-1, keepdims=True)
    h2 = x1 * lax.rsqrt(ms + EPS) * n2_ref[...]
    h2 = (h2 * (1.0 + sc_ref[...]) + sh_ref[...]).astype(BF16)

    row = lax.broadcasted_iota(jnp.int32, (tm, 1), 0)
    if has_state:
        t_in = row % seq_rows
    else:
        t_in = row
        first = (pl.program_id(0) % tiles_per_batch) == 0

        @pl.when(first)
        def _():
            carry_ref[...] = jnp.zeros_like(carry_ref)

    def conv(u, cols):
        r1 = pltpu.roll(u, 1, axis=0)
        r2 = pltpu.roll(u, 2, axis=0)
        if has_state:
            up_ref[:, cols] = u
            prev1 = jnp.where(t_in >= 1, r1, p1_ref[:, cols])
            prev2 = jnp.where(t_in >= 2, r2, p2_ref[:, cols])
        else:
            c6 = carry_ref[CARRY_ROWS - 2:CARRY_ROWS - 1, cols]
            c7 = carry_ref[CARRY_ROWS - 1:CARRY_ROWS, cols]
            prev1 = jnp.where(t_in >= 1, r1, c7)
            prev2 = jnp.where(t_in >= 2, r2, jnp.where(t_in == 0, c6, c7))
            carry_ref[:, cols] = u[tm - CARRY_ROWS:, :]
        return (cb_ref[:, cols] + cw_ref[0:1, cols] * prev2 + cw_ref[1:2, cols] * prev1
                + cw_ref[2:3, cols] * u)

    acc = jnp.zeros((tm, x1.shape[1]), F32)
    for f in range(d_ff // fc):
        gc = slice(f * fc, (f + 1) * fc)
        vc = slice(d_ff + f * fc, d_ff + (f + 1) * fc)
        cg = conv(jnp.dot(h2, wup_ref[:, gc], preferred_element_type=F32), gc)
        cv = conv(jnp.dot(h2, wup_ref[:, vc], preferred_element_type=F32), vc)
        act = (cg * jax.nn.sigmoid(cg) * cv).astype(BF16)
        acc = acc + jnp.dot(act, wdn_ref[gc, :], preferred_element_type=F32)
    y_ref[...] = x1 + g2_ref[...] * acc

    if not has_state:
        @pl.when((pl.program_id(0) % tiles_per_batch) == tiles_per_batch - 1)
        def _():
            st_ref[...] = carry_ref[...]


def _outffn(x2, osb, oml, mod, per_row, n_batch, seq_rows, tm, n2, wout_b, wup_b, conv_w, conv_b,
            wdn_b, state_rows):
    rows, d = x2.shape
    d_ff = wdn_b.shape[0]
    d_sb = osb.shape[1]
    d_mlp = oml.shape[1]
    tiles = rows // tm
    tpb = max(tiles // n_batch, 1)
    has_state = state_rows is not None
    const = lambda i: (0, 0)
    row_spec = lambda w: pl.BlockSpec((tm, w), lambda i: (i, 0))
    in_specs = [
        row_spec(d), row_spec(d_sb), row_spec(d_mlp),
        _mod_spec(per_row, 2, tm, d, tpb),
        _mod_spec(per_row, 3, tm, d, tpb),
        _mod_spec(per_row, 4, tm, d, tpb),
        _mod_spec(per_row, 5, tm, d, tpb),
        pl.BlockSpec((1, d), const),
        _resident(wout_b.shape),
        _resident(wup_b.shape),
        pl.BlockSpec(conv_w.shape, const),
        pl.BlockSpec((1, 2 * d_ff), const),
        _resident(wdn_b.shape),
    ]
    args = [x2, osb, oml, mod, mod, mod, mod, n2, wout_b, wup_b, conv_w, conv_b, wdn_b]
    if has_state:
        in_specs += [_resident((rows, 2 * d_ff)), _resident((rows, 2 * d_ff))]
        args += list(state_rows)
        out_specs = [row_spec(d), row_spec(2 * d_ff)]
        out_shape = [jax.ShapeDtypeStruct((rows, d), F32),
                     jax.ShapeDtypeStruct((rows, 2 * d_ff), F32)]
        scratch = []
    else:
        out_specs = [row_spec(d),
                     pl.BlockSpec((None, CARRY_ROWS, 2 * d_ff), lambda i: (i // tpb, 0, 0))]
        out_shape = [jax.ShapeDtypeStruct((rows, d), F32),
                     jax.ShapeDtypeStruct((n_batch, CARRY_ROWS, 2 * d_ff), F32)]
        scratch = [pltpu.VMEM((CARRY_ROWS, 2 * d_ff), F32)]
    kern = functools.partial(_outffn_kernel, tm=tm, d_ff=d_ff, fc=d_ff, seq_rows=seq_rows,
                             tiles_per_batch=tpb, has_state=has_state)
    return pl.pallas_call(
        kern, grid=(tiles,), in_specs=in_specs, out_specs=out_specs, out_shape=out_shape,
        scratch_shapes=scratch,
        compiler_params=_cparams(("arbitrary",)),
        name="outffn_sample" if has_state else "outffn_prompt",
    )(*args)


def _layer(l, yp, ys, c_prompt, c_sample, cache_k, cache_v, state_conv, page_table,
           norm1_g, norm2_g, w_ada, b_ada, w_in, q_norm_g, k_norm_g, sb_bias, v_norm_g,
           w_s, b_s, w_out, w_up, conv_w, conv_b, w_down):
    bp, tp, d = yp.shape
    db, ts, _ = ys.shape
    d_ff = w_down.shape[1]
    n_heads = sb_bias.shape[1]
    d_sb = n_heads * HEAD_DIM
    n_grp = v_norm_g.shape[1]
    d_mlp = n_grp * CHUNK
    n_pool, page = cache_k.shape[1], cache_k.shape[2]

    n_c = bp + db
    pad = (-n_c) % 8
    c_all = jnp.concatenate([c_prompt, c_sample, jnp.zeros((pad, d), F32)], axis=0)
    mod = _ada(c_all, w_ada[l], b_ada[l])
    mod_p = mod[:bp].reshape(bp, N_MOD, 1, d)
    mod_s = jnp.repeat(mod[bp:n_c].reshape(db, N_MOD, d).transpose(1, 0, 2), ts, axis=1)

    win_b = w_in[l].astype(BF16)
    wnat_b = jnp.concatenate([win_b[:, :d_sb], win_b[:, 2 * d_sb:]], axis=1)
    wkvt_b = win_b[:, d_sb:3 * d_sb].T
    wout_b = w_out[l].astype(BF16)
    wup_b = w_up[l].astype(BF16)
    wdn_b = w_down[l].astype(BF16)
    g1 = norm1_g[l].reshape(1, d)
    n2 = norm2_g[l].reshape(1, d)
    gq = (jnp.tile(q_norm_g[l], n_heads) * (LOG2E * HEAD_DIM ** -0.5)).reshape(1, d_sb)
    sbb2 = sb_bias[l] * LOG2E
    gk = jnp.tile(k_norm_g[l], n_heads).reshape(1, d_sb)
    gkc = jnp.broadcast_to(k_norm_g[l][:, None], (HEAD_DIM, LANES))
    gv = v_norm_g[l].reshape(1, d_mlp)
    qaux = jnp.where((jnp.arange(LANES) == HEAD_DIM)[None, :], sbb2[:, None], 0.0).astype(F32)
    cb = conv_b[l].reshape(1, 2 * d_ff)

    bmix_p = jnp.repeat(b_s[l].T, CHUNK, axis=1)
    xp2 = yp.reshape(bp * tp, d)
    kt, vt, qh, kh, vb, oml_p = _inproj_prompt(
        xp2, mod_p, bp, PROMPT_ROWS, wnat_b, wkvt_b, g1, gq, gkc, gv, qaux, w_s[l], bmix_p,
        d_sb, d_mlp)

    rows_s = db * ts
    bmix_s = jnp.repeat(jnp.tile(b_s[l][:, :ts], (1, CHUNK // ts)).T, CHUNK, axis=1)
    xs2 = ys.reshape(rows_s, d)
    ks_, vs_, qn_s, gv_s, oml_s = _inproj_sample(
        xs2, mod_s, ts, win_b, g1, gq, gk, gv, w_s[l], bmix_s, d_sb, d_mlp)
    bias_rows = jnp.tile(sbb2, ts).reshape(ts * n_heads, 1)
    cache_kt = cache_k[l].transpose(0, 2, 3, 1).reshape(n_pool, d_sb, page)
    cache_vt = cache_v[l].transpose(0, 2, 3, 1).reshape(n_pool, d_sb, page)
    osb_p, osb_s = _attn(qh, kh, vb.reshape(bp, tp, d_sb), page_table,
                         qn_s.reshape(db, ts, d_sb), ks_.reshape(db, ts, d_sb),
                         vs_.reshape(db, ts, d_sb), bias_rows, cache_kt, cache_vt)
    yp2, st_p = _outffn(xp2, osb_p.reshape(bp * tp, d_sb), oml_p, mod_p, False, bp, tp,
                        PROMPT_ROWS, n2,
                        wout_b, wup_b, conv_w[l], cb, wdn_b, None)
    st = state_conv[l]
    p1 = jnp.concatenate([st[:, 1:2], jnp.zeros((db, ts - 1, 2 * d_ff), F32)], axis=1)
    p2 = jnp.concatenate([st, jnp.zeros((db, ts - 2, 2 * d_ff), F32)], axis=1)
    ys2, up_s = _outffn(xs2, osb_s.reshape(rows_s, d_sb), oml_s, mod_s, True, db, ts, rows_s, n2,
                        wout_b, wup_b, conv_w[l], cb, wdn_b,
                        (p1.reshape(rows_s, 2 * d_ff), p2.reshape(rows_s, 2 * d_ff)))

    outs = (kt.reshape(bp, n_heads, HEAD_DIM, tp).transpose(0, 3, 1, 2),
            vt.reshape(bp, n_heads, HEAD_DIM, tp).transpose(0, 3, 1, 2),
            ks_.reshape(db, ts, n_heads, HEAD_DIM), vs_.reshape(db, ts, n_heads, HEAD_DIM),
            gv_s.reshape(db, ts, n_grp, CHUNK),
            st_p[:, CARRY_ROWS - (CONV_W - 1):],
            up_s.reshape(db, ts, 2 * d_ff)[:, ts - (CONV_W - 1):])
    return yp2.reshape(bp, tp, d), ys2.reshape(db, ts, d), outs


def kernel(x_prompt, x_sample, c_prompt, c_sample, cache_k, cache_v, state_conv, page_table,
           norm1_g, norm2_g, w_ada, b_ada, w_in, q_norm_g, k_norm_g, sb_bias, v_norm_g,
           w_s, b_s, w_out, w_up, conv_w, conv_b, w_down):
    depth = w_in.shape[0]
    yp, ys = x_prompt, x_sample
    per_layer = []
    for l in range(depth):
        yp, ys, outs = _layer(l, yp, ys, c_prompt, c_sample, cache_k, cache_v, state_conv,
                              page_table, norm1_g, norm2_g, w_ada, b_ada, w_in, q_norm_g,
                              k_norm_g, sb_bias, v_norm_g, w_s, b_s, w_out, w_up, conv_w,
                              conv_b, w_down)
        per_layer.append(outs)
    stacked = tuple(jnp.stack([o[i] for o in per_layer]) for i in range(7))
    return (yp, ys) + stacked
```

```python
import functools

import jax
import jax.numpy as jnp
from jax import lax
from jax.experimental import pallas as pl
from jax.experimental.pallas import tpu as pltpu

F32 = jnp.float32
BF16 = jnp.bfloat16

EPS = 1e-6
LOG2E = 1.4426950408889634
HEAD_DIM = 64
LANES = 128
CHUNK = 128
N_MOD = 6
CONV_W = 3
CARRY_ROWS = 8
BIAS_LANES = 2
ATTN_BLK = 256
PROMPT_ROWS = 512
SAMPLE_CHUNK_PAGES = 16
SAMPLE_RING_SLOTS = 3

VMEM_LIMIT = 56 * 1024 * 1024

NT_DIMS = (((1,), (1,)), ((), ()))


def _cparams(sem):
    return pltpu.CompilerParams(dimension_semantics=sem, vmem_limit_bytes=VMEM_LIMIT)


def _resident(shape):
    nd = len(shape)
    return pl.BlockSpec(shape, lambda *_: (0,) * nd, pipeline_mode=pl.Buffered(1))


def _mod_spec(per_row, k, tm, d, tiles_per_batch):
    if per_row:
        return pl.BlockSpec((None, tm, d), lambda i: (k, i, 0))
    return pl.BlockSpec((None, None, 1, d), lambda i: (i // tiles_per_batch, k, 0, 0))


def _ada_kernel(c_ref, w_ref, b_ref, o_ref):
    c = c_ref[...]
    s = c * jax.nn.sigmoid(c)
    o_ref[...] = jnp.dot(s, w_ref[...], preferred_element_type=F32,
                         precision=lax.Precision.HIGHEST) + b_ref[...]


def _ada(c_all, w_ada, b_ada):
    rows, d = c_all.shape
    n = w_ada.shape[1]
    bn = n // 4
    return pl.pallas_call(
        _ada_kernel,
        grid=(n // bn,),
        in_specs=[pl.BlockSpec((rows, d), lambda j: (0, 0)),
                  pl.BlockSpec((d, bn), lambda j: (0, j)),
                  pl.BlockSpec((1, bn), lambda j: (0, j))],
        out_specs=pl.BlockSpec((rows, bn), lambda j: (0, j)),
        out_shape=jax.ShapeDtypeStruct((rows, n), F32),
        compiler_params=_cparams(("arbitrary",)),
        name="ada",
    )(c_all, w_ada, b_ada.reshape(1, n))


def _two_head_norm(blk, g):
    lo = lax.broadcasted_iota(jnp.int32, blk.shape, 1) < HEAD_DIM
    sq = blk * blk
    s_lo = jnp.sum(jnp.where(lo, sq, 0.0), axis=-1, keepdims=True)
    s_hi = jnp.sum(jnp.where(lo, 0.0, sq), axis=-1, keepdims=True)
    r = lax.rsqrt(jnp.where(lo, s_lo, s_hi) * (1.0 / HEAD_DIM) + EPS)
    return blk * r * g


def _spatial_gate(proj, u0, g0, gv_ref, wmix_ref, bmix_ref, om_ref, gvo_ref, tm, n_grp, seq_rows):
    r_i = lax.broadcasted_iota(jnp.int32, (CHUNK, CHUNK), 0)
    c_i = lax.broadcasted_iota(jnp.int32, (CHUNK, CHUNK), 1)
    if seq_rows >= CHUNK:
        keep = r_i >= c_i
        mixing = lambda w: w
    else:
        keep = jnp.logical_and(r_i >= c_i, c_i >= (r_i // seq_rows) * seq_rows)
        pick = jnp.where(c_i == r_i % seq_rows, 1.0, 0.0)

        def mixing(w):
            rows_picked = jnp.dot(pick, w, preferred_element_type=F32,
                                  precision=lax.Precision.HIGHEST)
            return lax.dot_general(rows_picked, pick, NT_DIMS, preferred_element_type=F32,
                                   precision=lax.Precision.HIGHEST)
    for g in range(n_grp):
        cs = slice(g * CHUNK, (g + 1) * CHUNK)
        wm = jnp.where(keep, mixing(wmix_ref[g]), 0.0).astype(BF16)
        u = jax.nn.gelu(proj[:, u0 + g * CHUNK:u0 + (g + 1) * CHUNK])
        vg = jax.nn.gelu(proj[:, g0 + g * CHUNK:g0 + (g + 1) * CHUNK])
        vn = vg * lax.rsqrt(jnp.mean(vg * vg, axis=-1, keepdims=True) + EPS) * gv_ref[:, cs]
        if gvo_ref is not None:
            gvo_ref[:, cs] = vn
        vnb = vn.astype(BF16)
        for c in range(tm // CHUNK):
            rs = slice(c * CHUNK, (c + 1) * CHUNK)
            mixed = jnp.dot(wm, vnb[rs], preferred_element_type=F32) + bmix_ref[:, cs]
            om_ref[rs, cs] = (u[rs] * mixed).astype(BF16)


def _normed_input(x_ref, sh_ref, sc_ref, g_ref):
    x = x_ref[...]
    ms = jnp.mean(x * x, axis=-1, keepdims=True)
    h = x * lax.rsqrt(ms + EPS) * g_ref[...]
    return (h * (1.0 + sc_ref[...]) + sh_ref[...]).astype(BF16)


def _inproj_prompt_kernel(x_ref, sh_ref, sc_ref, g1_ref, wnat_ref, wkvt_ref, gq_ref, gkc_ref,
                          gv_ref, qaux_ref, wmix_ref, bmix_ref,
                          kt_ref, vt_ref, qh_ref, kh_ref, vb_ref, om_ref,
                          *, tm, d_sb, d_mlp, seq_rows):
    n_heads = d_sb // HEAD_DIM
    hb = _normed_input(x_ref, sh_ref, sc_ref, g1_ref)
    proj = jnp.dot(hb, wnat_ref[...], preferred_element_type=F32)
    kvt = lax.dot_general(wkvt_ref[...], hb, NT_DIMS, preferred_element_type=F32)

    gkc = jnp.tile(gkc_ref[...], (1, tm // LANES))
    kb = kh_ref.shape[-1]
    aux_rows = jnp.where(lax.broadcasted_iota(jnp.int32, (HEAD_DIM, kb), 0) < BIAS_LANES, 1.0, 0.0)
    for hd in range(n_heads):
        rs = slice(hd * HEAD_DIM, (hd + 1) * HEAD_DIM)
        blk = kvt[rs, :]
        kn = blk * lax.rsqrt(jnp.mean(blk * blk, axis=0, keepdims=True) + EPS) * gkc
        kt_ref[rs, :] = kn
        for c in range(tm // kb):
            kh_ref[hd, c, 0:HEAD_DIM, :] = kn[:, c * kb:(c + 1) * kb].astype(BF16)
            kh_ref[hd, c, HEAD_DIM:, :] = aux_rows.astype(BF16)
    vt_ref[...] = kvt[d_sb:, :]

    lo = lax.broadcasted_iota(jnp.int32, (tm, LANES), 1) < HEAD_DIM
    for p in range(d_sb // LANES):
        cs = slice(p * LANES, (p + 1) * LANES)
        qn = _two_head_norm(proj[:, p * LANES:(p + 1) * LANES], gq_ref[:, cs])
        vb_ref[:, cs] = proj[:, d_sb + p * LANES:d_sb + (p + 1) * LANES].astype(BF16)
        qh_ref[2 * p] = jnp.where(lo, qn, qaux_ref[2 * p:2 * p + 1, :]).astype(BF16)
        qr = pltpu.roll(qn, HEAD_DIM, axis=1)
        qh_ref[2 * p + 1] = jnp.where(lo, qr, qaux_ref[2 * p + 1:2 * p + 2, :]).astype(BF16)

    _spatial_gate(proj, 2 * d_sb, 2 * d_sb + d_mlp, gv_ref, wmix_ref, bmix_ref, om_ref, None,
                  tm, d_mlp // CHUNK, seq_rows)


def _inproj_sample_kernel(x_ref, sh_ref, sc_ref, g1_ref, win_ref, gq_ref, gk_ref, gv_ref,
                          wmix_ref, bmix_ref, k_ref, v_ref, qn_ref, gvo_ref, om_ref,
                          *, tm, d_sb, d_mlp, seq_rows):
    hb = _normed_input(x_ref, sh_ref, sc_ref, g1_ref)
    proj = jnp.dot(hb, win_ref[...], preferred_element_type=F32)
    for p in range(d_sb // LANES):
        cs = slice(p * LANES, (p + 1) * LANES)
        qn_ref[:, cs] = _two_head_norm(proj[:, p * LANES:(p + 1) * LANES], gq_ref[:, cs])
        k_ref[:, cs] = _two_head_norm(proj[:, d_sb + p * LANES:d_sb + (p + 1) * LANES],
                                      gk_ref[:, cs])
        v_ref[:, cs] = proj[:, 2 * d_sb + p * LANES:2 * d_sb + (p + 1) * LANES]
    _spatial_gate(proj, 3 * d_sb, 3 * d_sb + d_mlp, gv_ref, wmix_ref, bmix_ref, om_ref, gvo_ref,
                  tm, d_mlp // CHUNK, seq_rows)


def _inproj_prompt(x2, mod, n_batch, tm, wnat_b, wkvt_b, g1, gq, gkc, gv, qaux, wmix, bmix,
                   d_sb, d_mlp):
    rows, d = x2.shape
    n_heads = d_sb // HEAD_DIM
    t_len = rows // n_batch
    tpb = t_len // tm
    const = lambda i: (0, 0)
    in_specs = [
        pl.BlockSpec((tm, d), lambda i: (i, 0)),
        _mod_spec(False, 0, tm, d, tpb),
        _mod_spec(False, 1, tm, d, tpb),
        pl.BlockSpec((1, d), const),
        _resident(wnat_b.shape),
        _resident(wkvt_b.shape),
        pl.BlockSpec((1, d_sb), const),
        pl.BlockSpec((HEAD_DIM, LANES), const),
        pl.BlockSpec((1, d_mlp), const),
        pl.BlockSpec((n_heads, LANES), const),
        pl.BlockSpec(wmix.shape, lambda i: (0, 0, 0)),
        pl.BlockSpec((CHUNK, d_mlp), const),
    ]
    row_spec = lambda w: pl.BlockSpec((tm, w), lambda i: (i, 0))
    tmin_spec = pl.BlockSpec((None, d_sb, tm), lambda i: (i // tpb, 0, i % tpb))
    out_specs = [
        tmin_spec, tmin_spec,
        pl.BlockSpec((None, n_heads, tm, LANES), lambda i: (i // tpb, 0, i % tpb, 0)),
        pl.BlockSpec((None, n_heads, tm // ATTN_BLK, LANES, ATTN_BLK),
                     lambda i: (i // tpb, 0, i % tpb, 0, 0)),
        row_spec(d_sb), row_spec(d_mlp)]
    out_shape = [
        jax.ShapeDtypeStruct((n_batch, d_sb, t_len), F32),
        jax.ShapeDtypeStruct((n_batch, d_sb, t_len), F32),
        jax.ShapeDtypeStruct((n_batch, n_heads, t_len, LANES), BF16),
        jax.ShapeDtypeStruct((n_batch, n_heads, t_len // ATTN_BLK, LANES, ATTN_BLK), BF16),
        jax.ShapeDtypeStruct((rows, d_sb), BF16),
        jax.ShapeDtypeStruct((rows, d_mlp), BF16)]
    kern = functools.partial(_inproj_prompt_kernel, tm=tm, d_sb=d_sb, d_mlp=d_mlp, seq_rows=t_len)
    return pl.pallas_call(
        kern, grid=(rows // tm,), in_specs=in_specs, out_specs=out_specs, out_shape=out_shape,
        compiler_params=_cparams(("arbitrary",)), name="inproj_prompt",
    )(x2, mod, mod, g1, wnat_b, wkvt_b, gq, gkc, gv, qaux, wmix, bmix)


def _inproj_sample(x2, mod, seq_rows, win_b, g1, gq, gk, gv, wmix, bmix, d_sb, d_mlp):
    rows, d = x2.shape
    tm = rows
    const = lambda i: (0, 0)
    in_specs = [
        pl.BlockSpec((tm, d), lambda i: (i, 0)),
        _mod_spec(True, 0, tm, d, 1),
        _mod_spec(True, 1, tm, d, 1),
        pl.BlockSpec((1, d), const),
        _resident(win_b.shape),
        pl.BlockSpec((1, d_sb), const),
        pl.BlockSpec((1, d_sb), const),
        pl.BlockSpec((1, d_mlp), const),
        pl.BlockSpec(wmix.shape, lambda i: (0, 0, 0)),
        pl.BlockSpec((CHUNK, d_mlp), const),
    ]
    row_spec = lambda w: pl.BlockSpec((tm, w), lambda i: (i, 0))
    out_specs = [row_spec(d_sb), row_spec(d_sb), row_spec(d_sb), row_spec(d_mlp), row_spec(d_mlp)]
    out_shape = [jax.ShapeDtypeStruct((rows, d_sb), F32),
                 jax.ShapeDtypeStruct((rows, d_sb), F32),
                 jax.ShapeDtypeStruct((rows, d_sb), F32),
                 jax.ShapeDtypeStruct((rows, d_mlp), F32),
                 jax.ShapeDtypeStruct((rows, d_mlp), BF16)]
    kern = functools.partial(_inproj_sample_kernel, tm=tm, d_sb=d_sb, d_mlp=d_mlp,
                             seq_rows=seq_rows)
    return pl.pallas_call(
        kern, grid=(1,), in_specs=in_specs, out_specs=out_specs, out_shape=out_shape,
        compiler_params=_cparams(("arbitrary",)), name="inproj_sample",
    )(x2, mod, mod, g1, win_b, gq, gk, gv, wmix, bmix)


def _softplus2(z2):
    return jnp.maximum(z2, 0.0) + jnp.log(1.0 + jnp.exp2(-jnp.abs(z2))) * LOG2E


def _strict_upper(n):
    r = lax.broadcasted_iota(jnp.int32, (n, n), 0)
    c = lax.broadcasted_iota(jnp.int32, (n, n), 1)
    return jnp.where(r > c, 1.0, 0.0).astype(BF16)


def _sample_attention(pt_ref, q_ref, kn_ref, vn_ref, bias_ref, ck_hbm, cv_hbm, o_ref,
                      kbuf, vbuf, sem, qbd_ref, a_ref, acc_ref, acct_ref, car_ref,
                      *, n_seq, n_tok, n_heads, pages_per_chunk, n_chunks, n_buf, page, sub):
    total = n_seq * n_chunks
    d_sb = n_heads * HEAD_DIM
    rows = n_tok * n_heads
    ck = pages_per_chunk * page
    n_sub = ck // sub

    def chunk_copies(g, which):
        seq = g // n_chunks
        chunk = n_chunks - 1 - g % n_chunks
        slot = g % n_buf
        src, dst = ((ck_hbm, kbuf), (cv_hbm, vbuf))[which]
        cps = []
        for pg in range(pages_per_chunk):
            pid = pt_ref[seq, chunk * pages_per_chunk + pg]
            cols = pl.ds(pg * page, page)
            cps.append(pltpu.make_async_copy(src.at[pid], dst.at[slot, :, cols], sem.at[which, slot]))
        return cps

    def prime():
        for g0 in range(n_buf - 1):
            for cp in chunk_copies(g0, 0) + chunk_copies(g0, 1):
                cp.start()

    def own_mask():
        r_i = lax.broadcasted_iota(jnp.int32, (rows, d_sb), 0)
        c_i = lax.broadcasted_iota(jnp.int32, (rows, d_sb), 1)
        return (r_i % n_heads) == (c_i // HEAD_DIM)

    def begin_sequence(seq):
        q_nat = q_ref[seq]
        q_rep = jnp.concatenate(
            [jnp.broadcast_to(q_nat[t:t + 1, :], (n_heads, d_sb)) for t in range(n_tok)], axis=0)
        q_bd = jnp.where(own_mask(), q_rep, 0.0)
        qbd_ref[...] = q_bd.astype(BF16)
        bias = bias_ref[...]
        tok = lax.broadcasted_iota(jnp.int32, (rows, 1), 0) // n_heads
        acc = jnp.zeros((rows, d_sb), F32)
        car = jnp.zeros((rows, 1), F32)
        k_new = kn_ref[seq]
        v_new = vn_ref[seq]
        for s in range(n_tok - 1, -1, -1):
            z = jnp.sum(q_bd * k_new[s:s + 1, :], axis=-1, keepdims=True) + bias
            vis = tok > s
            sp = jnp.where(vis, _softplus2(z), 0.0)
            a = jnp.where(vis, jnp.exp2(z - sp - car), 0.0)
            acc = acc + a * v_new[s:s + 1, :]
            car = car + sp
        acc_ref[seq % 2] = acc
        acct_ref[seq % 2] = jnp.zeros(acct_ref.shape[1:], F32)
        car_ref[...] = jnp.broadcast_to(car, (rows, LANES))

    def end_sequence(seq):
        res = jnp.where(own_mask(), acc_ref[seq % 2] + acct_ref[seq % 2].T[0:rows, :], 0.0)
        o_ref[seq] = jnp.concatenate(
            [jnp.sum(res[t * n_heads:(t + 1) * n_heads], axis=0, keepdims=True)
             for t in range(n_tok)], axis=0)

    def k_logits(g, s):
        cols = pl.ds(s * sub, sub)
        return jnp.dot(qbd_ref[...], kbuf[g % n_buf, :, cols].astype(BF16),
                       preferred_element_type=F32) + bias_ref[...]

    def v_partial(g, s):
        cols = pl.ds(s * sub, sub)
        return lax.dot_general(vbuf[g % n_buf, :, cols].astype(BF16), a_ref[g % 2, :, cols],
                               NT_DIMS, preferred_element_type=F32)

    def k_weights(g, zs):
        upper = _strict_upper(sub)
        sps = [_softplus2(z) for z in zs]
        sums = [jnp.sum(sp, axis=-1, keepdims=True) for sp in sps]
        cr = car_ref[...]
        a_parts = [None] * n_sub
        later_all = jnp.dot(jnp.concatenate([sp.astype(BF16) for sp in sps], axis=0), upper,
                            preferred_element_type=F32)
        for s in range(n_sub - 1, -1, -1):
            later = later_all[s * rows:(s + 1) * rows]
            t = zs[s] - sps[s] - later - jnp.tile(cr, (1, sub // LANES))
            a_parts[s] = jnp.exp2(t).astype(BF16)
            cr = cr + sums[s]
        a_ref[g % 2] = jnp.concatenate(
            [jnp.concatenate(a_parts, axis=1), jnp.zeros((LANES - rows, ck), BF16)], axis=0)
        car_ref[...] = cr

    def k_side(g):
        k_weights(g, [k_logits(g, s) for s in range(n_sub)])

    def v_side(g, partials):
        tot = partials[0]
        for pv in partials[1:]:
            tot = tot + pv
        acct_ref[(g // n_chunks) % 2] += tot

    def step(g):
        has_k = g < total
        has_v = g >= 1

        @pl.when(has_k)
        def _():
            @pl.when(g % n_chunks == 0)
            def _():
                begin_sequence(g // n_chunks)

            for cp in chunk_copies(g, 0):
                cp.wait()

            @pl.when(g + n_buf - 1 < total)
            def _():
                for cp in chunk_copies(g + n_buf - 1, 0):
                    cp.start()

        @pl.when(has_v)
        def _():
            for cp in chunk_copies(g - 1, 1):
                cp.wait()

        @pl.when(jnp.logical_not(has_v))
        def _():
            k_side(g)

        @pl.when(jnp.logical_and(has_k, has_v))
        def _():
            partials, zs = [], []
            for s in range(n_sub):
                partials.append(v_partial(g - 1, s))
                zs.append(k_logits(g, s))
            v_side(g - 1, partials)
            k_weights(g, zs)

        @pl.when(jnp.logical_not(has_k))
        def _():
            v_side(g - 1, [v_partial(g - 1, s) for s in range(n_sub)])

        @pl.when(g + n_buf - 1 < total)
        def _():
            for cp in chunk_copies(g + n_buf - 1, 1):
                cp.start()

        @pl.when(jnp.logical_and(has_v, (g - 1) % n_chunks == n_chunks - 1))
        def _():
            end_sequence((g - 1) // n_chunks)

    return prime, step


def _attn_kernel(pt_ref, q_ref, k_ref, v_ref, qs_ref, kn_ref, vn_ref, bias_ref, ck_hbm, cv_hbm,
                 o_ref, os_ref,
                 acc_ref, car_ref, z_ref, kbuf, vbuf, sem, qbd_ref, sa_ref, sacc_ref, sacct_ref,
                 scar_ref, cnt_ref, *, blk, n_h, units_per_step, sample_cfg):
    b = pl.program_id(0)
    i = pl.program_id(1)
    prime, sample_step = _sample_attention(
        pt_ref, qs_ref, kn_ref, vn_ref, bias_ref, ck_hbm, cv_hbm, os_ref,
        kbuf, vbuf, sem, qbd_ref, sa_ref, sacc_ref, sacct_ref, scar_ref, **sample_cfg)
    n_sample_steps = sample_cfg["n_seq"] * sample_cfg["n_chunks"] + 1

    @pl.when(jnp.logical_and(b == 0, i == 0))
    def _():
        cnt_ref[0] = 0
        prime()

    def tick(k):
        assert k <= units_per_step
        u = cnt_ref[0]
        cnt_ref[0] = u + k
        g = u // units_per_step

        @pl.when(jnp.logical_and((u + k) // units_per_step > g, g < n_sample_steps))
        def _():
            sample_step(g)

    upper = _strict_upper(blk)
    reps = blk // LANES
    row = lax.broadcasted_iota(jnp.int32, (blk, blk), 0)
    col = lax.broadcasted_iota(jnp.int32, (blk, blk), 1)
    causal = col < row

    def logits(j):
        return [jnp.dot(q_ref[hh], k_ref[hh, j], preferred_element_type=F32) for hh in range(n_h)]

    def tiles(zs, j, mask, z_next_ref):
        heads = range(n_h)
        nxt = logits(jnp.maximum(j - 1, 0))
        for hh in heads:
            z_next_ref[hh] = nxt[hh]
        sps = [_softplus2(z) for z in zs]
        if mask is not None:
            sps = [jnp.where(mask, sp, 0.0) for sp in sps]
        lbs = [z - sp for z, sp in zip(zs, sps)]
        later_all = jnp.dot(jnp.concatenate([sp.astype(BF16) for sp in sps], axis=0), upper,
                            preferred_element_type=F32)
        laters = [later_all[hh * blk:(hh + 1) * blk] for hh in heads]
        cars = [car_ref[hh] for hh in heads]
        for hh in heads:
            car_ref[hh] = cars[hh] + jnp.sum(sps[hh], axis=-1, keepdims=True)
        a_s = [jnp.exp2(lbs[hh] - laters[hh] - jnp.tile(cars[hh], (1, reps))) for hh in heads]
        if mask is not None:
            a_s = [jnp.where(mask, a, 0.0) for a in a_s]
        for p in range(n_h // 2):
            v = v_ref[pl.ds(pl.multiple_of(j * blk, blk), blk), p * LANES:(p + 1) * LANES]
            a_pair = jnp.concatenate([a_s[2 * p].astype(BF16), a_s[2 * p + 1].astype(BF16)], axis=0)
            av = jnp.dot(a_pair, v, preferred_element_type=F32)
            acc_ref[2 * p] += av[:blk]
            acc_ref[2 * p + 1] += av[blk:]

    acc_ref[...] = jnp.zeros_like(acc_ref)
    car_ref[...] = jnp.zeros_like(car_ref)
    tiles(logits(i), i, causal, z_ref)
    tick(1)

    def body(n, c):
        tiles([z_ref[hh] for hh in range(n_h)], i - 1 - n, None, z_ref)
        tick(1)
        return c

    lax.fori_loop(0, i, body, 0)

    lo = lax.broadcasted_iota(jnp.int32, (blk, LANES), 1) < HEAD_DIM
    for p in range(n_h // 2):
        o_ref[:, p * LANES:(p + 1) * LANES] = jnp.where(
            lo, acc_ref[2 * p], acc_ref[2 * p + 1]).astype(o_ref.dtype)


def _attn(qh, kh, vb, page_table, q_nat, k_new, v_new, bias_rows, cache_kt, cache_vt):
    n_batch, n_heads, t_len, _ = qh.shape
    blk = kh.shape[-1]
    n_blk = t_len // blk
    d_sb = n_heads * HEAD_DIM
    n_seq, n_tok, _ = q_nat.shape
    n_pages = page_table.shape[1]
    page = cache_kt.shape[2]
    pages_per_chunk = SAMPLE_CHUNK_PAGES
    n_buf = SAMPLE_RING_SLOTS
    n_chunks = n_pages // pages_per_chunk
    assert n_chunks * pages_per_chunk == n_pages
    rows = n_tok * n_heads
    ck = pages_per_chunk * page
    units = n_batch * n_blk * (n_blk + 1) // 2
    units_per_step = units // (n_seq * n_chunks + 1)
    assert units_per_step >= 1, "not enough prompt work to pace the sample steps"
    sample_cfg = dict(n_seq=n_seq, n_tok=n_tok, n_heads=n_heads, pages_per_chunk=pages_per_chunk,
                      n_chunks=n_chunks, n_buf=n_buf, page=page, sub=blk)
    whole = lambda shape: pl.BlockSpec(shape, lambda b, i, pt: (0,) * len(shape))
    grid_spec = pltpu.PrefetchScalarGridSpec(
        num_scalar_prefetch=1,
        grid=(n_batch, n_blk),
        in_specs=[pl.BlockSpec((None, n_heads, blk, LANES), lambda b, i, pt: (b, 0, i, 0)),
                  pl.BlockSpec((None, n_heads, n_blk, LANES, blk), lambda b, i, pt: (b, 0, 0, 0, 0),
                               pipeline_mode=pl.Buffered(1)),
                  pl.BlockSpec((None, t_len, d_sb), lambda b, i, pt: (b, 0, 0),
                               pipeline_mode=pl.Buffered(1)),
                  whole((n_seq, n_tok, d_sb)), whole((n_seq, n_tok, d_sb)),
                  whole((n_seq, n_tok, d_sb)), whole((rows, 1)),
                  pl.BlockSpec(memory_space=pl.ANY),
                  pl.BlockSpec(memory_space=pl.ANY)],
        out_specs=[pl.BlockSpec((None, blk, d_sb), lambda b, i, pt: (b, i, 0)),
                   whole((n_seq, n_tok, d_sb))],
        scratch_shapes=[pltpu.VMEM((n_heads, blk, LANES), F32),
                        pltpu.VMEM((n_heads, blk, LANES), F32),
                        pltpu.VMEM((n_heads, blk, blk), F32),
                        pltpu.VMEM((n_buf, d_sb, ck), F32),
                        pltpu.VMEM((n_buf, d_sb, ck), F32),
                        pltpu.SemaphoreType.DMA((2, n_buf)),
                        pltpu.VMEM((rows, d_sb), BF16),
                        pltpu.VMEM((2, LANES, ck), BF16),
                        pltpu.VMEM((2, rows, d_sb), F32),
                        pltpu.VMEM((2, d_sb, LANES), F32),
                        pltpu.VMEM((rows, LANES), F32),
                        pltpu.SMEM((1,), jnp.int32)],
    )
    kern = functools.partial(_attn_kernel, blk=blk, n_h=n_heads, units_per_step=units_per_step,
                             sample_cfg=sample_cfg)
    return pl.pallas_call(
        kern, grid_spec=grid_spec,
        out_shape=[jax.ShapeDtypeStruct((n_batch, t_len, d_sb), BF16),
                   jax.ShapeDtypeStruct((n_seq, n_tok, d_sb), F32)],
        compiler_params=_cparams(("arbitrary", "arbitrary")),
        name="attn",
    )(page_table, qh, kh, vb, q_nat, k_new, v_new, bias_rows, cache_kt, cache_vt)


def _outffn_kernel(*refs, tm, d_ff, fc, seq_rows, tiles_per_batch, has_state):
    if has_state:
        (x_ref, osb_ref, oml_ref, g1_ref, sh_ref, sc_ref, g2_ref, n2_ref, wout_ref, wup_ref,
         cw_ref, cb_ref, wdn_ref, p1_ref, p2_ref, y_ref, up_ref) = refs
    else:
        (x_ref, osb_ref, oml_ref, g1_ref, sh_ref, sc_ref, g2_ref, n2_ref, wout_ref, wup_ref,
         cw_ref, cb_ref, wdn_ref, y_ref, st_ref, carry_ref) = refs
    d_sb = osb_ref.shape[1]
    mix = (jnp.dot(osb_ref[...].astype(BF16), wout_ref[0:d_sb, :], preferred_element_type=F32)
           + jnp.dot(oml_ref[...], wout_ref[d_sb:, :], preferred_element_type=F32))
    x1 = x_ref[...] + g1_ref[...] * mix
    ms = jnp.mean(x1 * x1, axis=-1, keepdims=True)
    h2 = x1 * lax.rsqrt(ms + EPS) * n2_ref[...]
    h2 = (h2 * (1.0 + sc_ref[...]) + sh_ref[...]).astype(BF16)

    row = lax.broadcasted_iota(jnp.int32, (tm, 1), 0)
    if has_state:
        t_in = row % seq_rows
    else:
        t_in = row
        first = (pl.program_id(0) % tiles_per_batch) == 0

        @pl.when(first)
        def _():
            carry_ref[...] = jnp.zeros_like(carry_ref)

    def conv(u, cols):
        r1 = pltpu.roll(u, 1, axis=0)
        r2 = pltpu.roll(u, 2, axis=0)
        if has_state:
            up_ref[:, cols] = u
            prev1 = jnp.where(t_in >= 1, r1, p1_ref[:, cols])
            prev2 = jnp.where(t_in >= 2, r2, p2_ref[:, cols])
        else:
            c6 = carry_ref[CARRY_ROWS - 2:CARRY_ROWS - 1, cols]
            c7 = carry_ref[CARRY_ROWS - 1:CARRY_ROWS, cols]
            prev1 = jnp.where(t_in >= 1, r1, c7)
            prev2 = jnp.where(t_in >= 2, r2, jnp.where(t_in == 0, c6, c7))
            carry_ref[:, cols] = u[tm - CARRY_ROWS:, :]
        return (cb_ref[:, cols] + cw_ref[0:1, cols] * prev2 + cw_ref[1:2, cols] * prev1
                + cw_ref[2:3, cols] * u)

    acc = jnp.zeros((tm, x1.shape[1]), F32)
    for f in range(d_ff // fc):
        gc = slice(f * fc, (f + 1) * fc)
        vc = slice(d_ff + f * fc, d_ff + (f + 1) * fc)
        cg = conv(jnp.dot(h2, wup_ref[:, gc], preferred_element_type=F32), gc)
        cv = conv(jnp.dot(h2, wup_ref[:, vc], preferred_element_type=F32), vc)
        act = (cg * jax.nn.sigmoid(cg) * cv).astype(BF16)
        acc = acc + jnp.dot(act, wdn_ref[gc, :], preferred_element_type=F32)
    y_ref[...] = x1 + g2_ref[...] * acc

    if not has_state:
        @pl.when((pl.program_id(0) % tiles_per_batch) == tiles_per_batch - 1)
        def _():
            st_ref[...] = carry_ref[...]


def _outffn(x2, osb, oml, mod, per_row, n_batch, seq_rows, tm, n2, wout_b, wup_b, conv_w, conv_b,
            wdn_b, state_rows):
    rows, d = x2.shape
    d_ff = wdn_b.shape[0]
    d_sb = osb.shape[1]
    d_mlp = oml.shape[1]
    tiles = rows // tm
    tpb = max(tiles // n_batch, 1)
    has_state = state_rows is not None
    const = lambda i: (0, 0)
    row_spec = lambda w: pl.BlockSpec((tm, w), lambda i: (i, 0))
    in_specs = [
        row_spec(d), row_spec(d_sb), row_spec(d_mlp),
        _mod_spec(per_row, 2, tm, d, tpb),
        _mod_spec(per_row, 3, tm, d, tpb),
        _mod_spec(per_row, 4, tm, d, tpb),
        _mod_spec(per_row, 5, tm, d, tpb),
        pl.BlockSpec((1, d), const),
        _resident(wout_b.shape),
        _resident(wup_b.shape),
        pl.BlockSpec(conv_w.shape, const),
        pl.BlockSpec((1, 2 * d_ff), const),
        _resident(wdn_b.shape),
    ]
    args = [x2, osb, oml, mod, mod, mod, mod, n2, wout_b, wup_b, conv_w, conv_b, wdn_b]
    if has_state:
        in_specs += [_resident((rows, 2 * d_ff)), _resident((rows, 2 * d_ff))]
        args += list(state_rows)
        out_specs = [row_spec(d), row_spec(2 * d_ff)]
        out_shape = [jax.ShapeDtypeStruct((rows, d), F32),
                     jax.ShapeDtypeStruct((rows, 2 * d_ff), F32)]
        scratch = []
    else:
        out_specs = [row_spec(d),
                     pl.BlockSpec((None, CARRY_ROWS, 2 * d_ff), lambda i: (i // tpb, 0, 0))]
        out_shape = [jax.ShapeDtypeStruct((rows, d), F32),
                     jax.ShapeDtypeStruct((n_batch, CARRY_ROWS, 2 * d_ff), F32)]
        scratch = [pltpu.VMEM((CARRY_ROWS, 2 * d_ff), F32)]
    kern = functools.partial(_outffn_kernel, tm=tm, d_ff=d_ff, fc=d_ff, seq_rows=seq_rows,
                             tiles_per_batch=tpb, has_state=has_state)
    return pl.pallas_call(
        kern, grid=(tiles,), in_specs=in_specs, out_specs=out_specs, out_shape=out_shape,
        scratch_shapes=scratch,
        compiler_params=_cparams(("arbitrary",)),
        name="outffn_sample" if has_state else "outffn_prompt",
    )(*args)


def _layer(l, yp, ys, c_prompt, c_sample, cache_k, cache_v, state_conv, page_table,
           norm1_g, norm2_g, w_ada, b_ada, w_in, q_norm_g, k_norm_g, sb_bias, v_norm_g,
           w_s, b_s, w_out, w_up, conv_w, conv_b, w_down):
    bp, tp, d = yp.shape
    db, ts, _ = ys.shape
    d_ff = w_down.shape[1]
    n_heads = sb_bias.shape[1]
    d_sb = n_heads * HEAD_DIM
    n_grp = v_norm_g.shape[1]
    d_mlp = n_grp * CHUNK
    n_pool, page = cache_k.shape[1], cache_k.shape[2]

    n_c = bp + db
    pad = (-n_c) % 8
    c_all = jnp.concatenate([c_prompt, c_sample, jnp.zeros((pad, d), F32)], axis=0)
    mod = _ada(c_all, w_ada[l], b_ada[l])
    mod_p = mod[:bp].reshape(bp, N_MOD, 1, d)
    mod_s = jnp.repeat(mod[bp:n_c].reshape(db, N_MOD, d).transpose(1, 0, 2), ts, axis=1)

    win_b = w_in[l].astype(BF16)
    wnat_b = jnp.concatenate([win_b[:, :d_sb], win_b[:, 2 * d_sb:]], axis=1)
    wkvt_b = win_b[:, d_sb:3 * d_sb].T
    wout_b = w_out[l].astype(BF16)
    wup_b = w_up[l].astype(BF16)
    wdn_b = w_down[l].astype(BF16)
    g1 = norm1_g[l].reshape(1, d)
    n2 = norm2_g[l].reshape(1, d)
    gq = (jnp.tile(q_norm_g[l], n_heads) * (LOG2E * HEAD_DIM ** -0.5)).reshape(1, d_sb)
    sbb2 = sb_bias[l] * LOG2E
    gk = jnp.tile(k_norm_g[l], n_heads).reshape(1, d_sb)
    gkc = jnp.broadcast_to(k_norm_g[l][:, None], (HEAD_DIM, LANES))
    gv = v_norm_g[l].reshape(1, d_mlp)
    sbb2_hi = sbb2.astype(BF16).astype(F32)
    lane = jnp.arange(LANES)[None, :]
    qaux = (jnp.where(lane == HEAD_DIM, sbb2_hi[:, None], 0.0)
            + jnp.where(lane == HEAD_DIM + 1, (sbb2 - sbb2_hi)[:, None], 0.0)).astype(F32)
    cb = conv_b[l].reshape(1, 2 * d_ff)

    bmix_p = jnp.repeat(b_s[l].T, CHUNK, axis=1)
    xp2 = yp.reshape(bp * tp, d)
    kt, vt, qh, kh, vb, oml_p = _inproj_prompt(
        xp2, mod_p, bp, PROMPT_ROWS, wnat_b, wkvt_b, g1, gq, gkc, gv, qaux, w_s[l], bmix_p,
        d_sb, d_mlp)

    rows_s = db * ts
    bmix_s = jnp.repeat(jnp.tile(b_s[l][:, :ts], (1, CHUNK // ts)).T, CHUNK, axis=1)
    xs2 = ys.reshape(rows_s, d)
    ks_, vs_, qn_s, gv_s, oml_s = _inproj_sample(
        xs2, mod_s, ts, win_b, g1, gq, gk, gv, w_s[l], bmix_s, d_sb, d_mlp)
    bias_rows = jnp.tile(sbb2, ts).reshape(ts * n_heads, 1)
    cache_kt = cache_k[l].transpose(0, 2, 3, 1).reshape(n_pool, d_sb, page)
    cache_vt = cache_v[l].transpose(0, 2, 3, 1).reshape(n_pool, d_sb, page)
    osb_p, osb_s = _attn(qh, kh, vb.reshape(bp, tp, d_sb), page_table,
                         qn_s.reshape(db, ts, d_sb), ks_.reshape(db, ts, d_sb),
                         vs_.reshape(db, ts, d_sb), bias_rows, cache_kt, cache_vt)
    yp2, st_p = _outffn(xp2, osb_p.reshape(bp * tp, d_sb), oml_p, mod_p, False, bp, tp,
                        PROMPT_ROWS, n2,
                        wout_b, wup_b, conv_w[l], cb, wdn_b, None)
    st = state_conv[l]
    p1 = jnp.concatenate([st[:, 1:2], jnp.zeros((db, ts - 1, 2 * d_ff), F32)], axis=1)
    p2 = jnp.concatenate([st, jnp.zeros((db, ts - 2, 2 * d_ff), F32)], axis=1)
    ys2, up_s = _outffn(xs2, osb_s.reshape(rows_s, d_sb), oml_s, mod_s, True, db, ts, rows_s, n2,
                        wout_b, wup_b, conv_w[l], cb, wdn_b,
                        (p1.reshape(rows_s, 2 * d_ff), p2.reshape(rows_s, 2 * d_ff)))

    outs = (kt.reshape(bp, n_heads, HEAD_DIM, tp).transpose(0, 3, 1, 2),
            vt.reshape(bp, n_heads, HEAD_DIM, tp).transpose(0, 3, 1, 2),
            ks_.reshape(db, ts, n_heads, HEAD_DIM), vs_.reshape(db, ts, n_heads, HEAD_DIM),
            gv_s.reshape(db, ts, n_grp, CHUNK),
            st_p[:, CARRY_ROWS - (CONV_W - 1):],
            up_s.reshape(db, ts, 2 * d_ff)[:, ts - (CONV_W - 1):])
    return yp2.reshape(bp, tp, d), ys2.reshape(db, ts, d), outs


def kernel(x_prompt, x_sample, c_prompt, c_sample, cache_k, cache_v, state_conv, page_table,
           norm1_g, norm2_g, w_ada, b_ada, w_in, q_norm_g, k_norm_g, sb_bias, v_norm_g,
           w_s, b_s, w_out, w_up, conv_w, conv_b, w_down):
    depth = w_in.shape[0]
    yp, ys = x_prompt, x_sample
    per_layer = []
    for l in range(depth):
        yp, ys, outs = _layer(l, yp, ys, c_prompt, c_sample, cache_k, cache_v, state_conv,
                              page_table, norm1_g, norm2_g, w_ada, b_ada, w_in, q_norm_g,
                              k_norm_g, sb_bias, v_norm_g, w_s, b_s, w_out, w_up, conv_w,
                              conv_b, w_down)
        per_layer.append(outs)
    stacked = tuple(jnp.stack([o[i] for o in per_layer]) for i in range(7))
    return (yp, ys) + stacked
```
